```python
import math, functools
import jax, jax.numpy as jnp
from jax import lax
import numpy as np

D_MODEL = 2048
BATCH = 8
SEQ = 2048
DEPTH = 2
DEC_BATCH = 128
DEC_SEQ = 4
PAST_LEN = 8192
PAGE_SIZE = 128

N_A_LAYERS = DEPTH // 2
N_B_LAYERS = DEPTH - N_A_LAYERS
A_HEADS = 16
A_HEAD_DIM = D_MODEL // A_HEADS
A_KV_GROUPS = 2
A_REP = A_HEADS // A_KV_GROUPS
CMP_LEN = 32
CMP_STRIDE = 16
CMP_RATIO = CMP_LEN // CMP_STRIDE
SEL_BLOCK = 64
SEL_TOPK = 16
N_LOCAL_FORCED = 2
A_WINDOW = 512
SEL_QUERY_BLOCK = 16
A_IN = A_HEADS * A_HEAD_DIM + 6 * A_KV_GROUPS * A_HEAD_DIM + 3 * A_HEADS
B_HEADS = 32
B_HEAD_DIM = D_MODEL // B_HEADS
B_KV_HEADS = 4
B_REP = B_HEADS // B_KV_HEADS
B_WINDOW = 128
B_KV_OUT = 2 * B_KV_HEADS * B_HEAD_DIM
D_FF = 4 * D_MODEL
BAND_BLOCK = 128
NORM_EPS = 1e-6
NEG_INF = -1e30
FORCE_BONUS = 1e4

kernel_name = 'yoco_nsa_swa_sink_decoder_step'


def rmsnorm(x, g):
    xf = x.astype(jnp.float32)
    y = xf * lax.rsqrt(jnp.mean(xf * xf, axis=-1, keepdims=True) + NORM_EPS)
    return (y * g.astype(jnp.float32)).astype(x.dtype)


def alibi_slopes(n_heads):
    return jnp.asarray(2.0 ** (-8.0 * np.arange(1, n_heads + 1) / n_heads), dtype=jnp.float32)


def sq_relu_mlp(x, w1, w2):
    return jnp.square(jax.nn.relu(x @ w1)) @ w2


def attn_probs(q, k, q_pos, k_pos, valid, slopes, sinks=None):
    scale = q.shape[-1] ** -0.5
    s = jnp.einsum('bqgrd,bkgd->bgrqk', q, k).astype(jnp.float32) * scale
    dist = (q_pos[:, None] - k_pos[None, :]).astype(jnp.float32)
    s = jnp.where(valid, s - slopes[:, :, None, None] * dist, NEG_INF)
    if sinks is None:
        return jax.nn.softmax(s, axis=-1)
    sk = sinks.astype(jnp.float32)[None, :, :, None, None]
    m = jnp.maximum(jnp.max(s, axis=-1, keepdims=True), sk)
    e = jnp.exp(s - m)
    return e / (jnp.sum(e, axis=-1, keepdims=True) + jnp.exp(sk - m))


def attend(q, k, v, q_pos, k_pos, valid, slopes, sinks=None):
    p = attn_probs(q, k, q_pos, k_pos, valid, slopes, sinks)
    return jnp.einsum('bgrqk,bkgd->bqgrd', p.astype(v.dtype), v)


def banded_attention(q, k, v, window, slopes, sinks=None):
    bsz, t_len = q.shape[:2]
    nqb = t_len // BAND_BLOCK
    nband = window // BAND_BLOCK + 1

    def bands(a):
        a = jnp.pad(a, ((0, 0), (window, 0), (0, 0), (0, 0)))
        a = a.reshape(bsz, nqb + nband - 1, BAND_BLOCK, *a.shape[2:])
        return jnp.concatenate([a[:, j:j + nqb] for j in range(nband)], axis=2)

    kb, vb = bands(k), bands(v)
    qb = q.reshape(bsz, nqb, BAND_BLOCK, *q.shape[2:])
    q_pos = jnp.arange(t_len).reshape(nqb, BAND_BLOCK)
    k_pos = jnp.arange(nqb)[:, None] * BAND_BLOCK - window + jnp.arange(nband * BAND_BLOCK)[None, :]
    d = q_pos[:, :, None] - k_pos[:, None, :]
    valid = (k_pos[:, None, :] >= 0) & (d >= 0) & (d < window)
    f = functools.partial(attend, slopes=slopes, sinks=sinks)
    o = jax.vmap(f, in_axes=(1, 1, 1, 0, 0, 0), out_axes=1)(qb, kb, vb, q_pos, k_pos, valid)
    return o.reshape(q.shape)


def window_attention_cached(q, q_pos, k_all, v_all, k_pos, window, slopes, sinks=None):
    d = q_pos[:, None] - k_pos[None, :]
    valid = (d >= 0) & (d < window)
    return attend(q, k_all, v_all, q_pos, k_pos, valid, slopes, sinks)


def compress(x, pe, w1, w2):
    bsz, t_len = x.shape[:2]
    nc = t_len // CMP_STRIDE
    n_cmp = nc - CMP_RATIO + 1
    chunks = x[:, :nc * CMP_STRIDE].reshape(bsz, nc, CMP_STRIDE, *x.shape[2:])
    w1r = w1.reshape(CMP_RATIO, CMP_STRIDE, *w1.shape[1:])
    per = pe.reshape(CMP_RATIO, CMP_STRIDE, -1)
    h = 0
    for r in range(CMP_RATIO):
        part = jnp.einsum('bcsgd,sde->bcge', chunks, w1r[r]) + jnp.einsum('sd,sde->e', per[r], w1r[r])
        h = h + part[:, r:r + n_cmp]
    return jnp.einsum('bcge,ef->bcgf', jax.nn.gelu(h), w2)


def cmp_to_sel_overlap(n_cmp, n_sel):
    start = np.arange(n_cmp) * CMP_STRIDE
    end = start + CMP_LEN - 1
    s0 = np.arange(n_sel) * SEL_BLOCK
    s1 = s0 + SEL_BLOCK - 1
    m = (start[:, None] <= s1[None, :]) & (end[:, None] >= s0[None, :])
    return jnp.asarray(m, dtype=jnp.float32)


def nsa_project(xn, w_in, q_gain, k_gain):
    bsz, t = xn.shape[:2]
    hq = A_HEADS * A_HEAD_DIM
    hkv = 6 * A_KV_GROUPS * A_HEAD_DIM
    z = xn @ w_in
    q = rmsnorm(z[..., :hq].reshape(bsz, t, A_KV_GROUPS, A_REP, A_HEAD_DIM), q_gain)
    kv = z[..., hq:hq + hkv].reshape(bsz, t, 6, A_KV_GROUPS, A_HEAD_DIM)
    gates = jax.nn.sigmoid(z[..., hq + hkv:].reshape(bsz, t, 3, A_KV_GROUPS, A_REP))
    k_sel = rmsnorm(kv[:, :, 2], k_gain[1])
    k_win = rmsnorm(kv[:, :, 4], k_gain[2])
    rows = jnp.stack([kv[:, :, 0], kv[:, :, 1], k_sel, kv[:, :, 3]], axis=2)
    win = jnp.stack([k_win, kv[:, :, 5]], axis=2)
    return q, gates, rows, win


def nsa_cmp_branch(q, q_pos, k_raw, v_raw, k_gain, pe, w1, w2, slopes):
    kc = rmsnorm(compress(k_raw, pe[0], w1[0], w2[0]), k_gain)
    vc = compress(v_raw, pe[1], w1[1], w2[1])
    end = jnp.arange(kc.shape[1]) * CMP_STRIDE + (CMP_LEN - 1)
    valid = end[None, :] <= q_pos[:, None]
    any_valid = jnp.any(valid, axis=-1, keepdims=True).astype(jnp.float32)
    p = attn_probs(q, kc, q_pos, end, valid, slopes) * any_valid
    o = jnp.einsum('bgrqk,bkgd->bqgrd', p.astype(vc.dtype), vc)
    return o, p


def nsa_select(p_cmp, q_pos, n_sel):
    imp = jnp.einsum('bgrqc,cj->bqgj', p_cmp, cmp_to_sel_overlap(p_cmp.shape[-1], n_sel))
    blk = jnp.arange(n_sel)[None, :]
    cur = (q_pos // SEL_BLOCK)[:, None]
    valid = blk <= cur
    forced = (blk == 0) | (((cur - blk) < N_LOCAL_FORCED) & valid)
    score = jnp.where(valid[None, :, None, :], imp + FORCE_BONUS * forced.astype(jnp.float32)[None, :, None, :], NEG_INF)
    _, idx = lax.top_k(score, min(SEL_TOPK, n_sel))
    return idx


def nsa_sel_branch(q, q_pos, k_full, v_full, idx, slopes):
    bsz, t_k = k_full.shape[:2]
    n_sel = -(-t_k // SEL_BLOCK)
    padlen = n_sel * SEL_BLOCK - t_k

    def blocks(a):
        return jnp.pad(a, ((0, 0), (0, padlen), (0, 0), (0, 0))).reshape(bsz, n_sel, SEL_BLOCK, *a.shape[2:])

    kb, vb = blocks(k_full), blocks(v_full)
    scale = q.shape[-1] ** -0.5
    bi = jnp.arange(bsz)[:, None, None, None]
    gi = jnp.arange(kb.shape[3])[None, None, :, None]

    def one(args):
        qc, pc, ic = args
        kg = kb[bi, ic, :, gi, :]
        vg = vb[bi, ic, :, gi, :]
        kpos = ic[..., None] * SEL_BLOCK + jnp.arange(SEL_BLOCK)
        dist = (pc[None, :, None, None, None] - kpos).astype(jnp.float32)[:, :, :, None]
        s = jnp.einsum('bqgrd,bqgnld->bqgrnl', qc, kg).astype(jnp.float32) * scale
        s = jnp.where(dist >= 0, s - slopes[None, None, :, :, None, None] * dist, NEG_INF)
        sh = s.shape
        p = jax.nn.softmax(s.reshape(*sh[:4], -1), axis=-1).reshape(sh)
        return jnp.einsum('bqgrnl,bqgnld->bqgrd', p.astype(vg.dtype), vg)

    t_q = q.shape[1]
    qblk = SEL_QUERY_BLOCK if t_q % SEL_QUERY_BLOCK == 0 else t_q
    nq = t_q // qblk
    if nq == 1:
        return one((q, q_pos, idx))
    to_blocks = lambda a: jnp.moveaxis(a.reshape(bsz, nq, qblk, *a.shape[2:]), 1, 0)
    o = lax.map(one, (to_blocks(q), q_pos.reshape(nq, qblk), to_blocks(idx)))
    return jnp.moveaxis(o, 0, 1).reshape(q.shape)


def nsa_cmp_sel(q, q_pos, full_rows, k_gain, pe, w1, w2, slopes):
    o_cmp, p_cmp = nsa_cmp_branch(q, q_pos, full_rows[:, :, 0], full_rows[:, :, 1], k_gain, pe, w1, w2, slopes)
    idx = nsa_select(p_cmp, q_pos, -(-full_rows.shape[1] // SEL_BLOCK))
    o_sel = nsa_sel_branch(q, q_pos, full_rows[:, :, 2], full_rows[:, :, 3], idx, slopes)
    return o_cmp, o_sel


def nsa_combine(o_cmp, o_sel, o_win, gates, w_out):
    o = gates[:, :, 0, ..., None] * o_cmp + gates[:, :, 1, ..., None] * o_sel + gates[:, :, 2, ..., None] * o_win
    return o.reshape(*o.shape[:2], -1) @ w_out


def nsa_prompt(xn, w_in, q_gain, k_gain, pe, w1, w2, w_out, slopes):
    q, gates, rows, win = nsa_project(xn, w_in, q_gain, k_gain)
    t_len = xn.shape[1]
    pos = jnp.arange(t_len)
    o_cmp, o_sel = nsa_cmp_sel(q, pos, rows, k_gain[0], pe, w1, w2, slopes)
    o_win = banded_attention(q, win[:, :, 0], win[:, :, 1], A_WINDOW, slopes)
    y = nsa_combine(o_cmp, o_sel, o_win, gates, w_out)
    return y, rows, win[:, -min(A_WINDOW, t_len):]


def nsa_sample(xn, cache_kv, page_table, cache_win, w_in, q_gain, k_gain, pe, w1, w2, w_out, slopes):
    q, gates, rows, win = nsa_project(xn, w_in, q_gain, k_gain)
    db, t_new = xn.shape[:2]
    past = cache_kv[page_table].reshape(db, -1, *cache_kv.shape[2:])
    past_len = past.shape[1]
    pos = past_len + jnp.arange(t_new)
    full = jnp.concatenate([past, rows], axis=1)
    o_cmp, o_sel = nsa_cmp_sel(q, pos, full, k_gain[0], pe, w1, w2, slopes)
    win_all = jnp.concatenate([cache_win, win], axis=1)
    k_pos = past_len - cache_win.shape[1] + jnp.arange(win_all.shape[1])
    o_win = window_attention_cached(q, pos, win_all[:, :, 0], win_all[:, :, 1], k_pos, A_WINDOW, slopes)
    y = nsa_combine(o_cmp, o_sel, o_win, gates, w_out)
    return y, rows, win_all[:, -cache_win.shape[1]:]


def shared_kv(h, norm_g, w_kv, k_gain):
    bsz, t = h.shape[:2]
    z = (rmsnorm(h, norm_g) @ w_kv).reshape(bsz, t, 2, B_KV_HEADS, B_HEAD_DIM)
    return jnp.stack([rmsnorm(z[:, :, 0], k_gain), z[:, :, 1]], axis=2)


def swa_query(xn, w_q, q_gain):
    bsz, t = xn.shape[:2]
    return rmsnorm((xn @ w_q).reshape(bsz, t, B_KV_HEADS, B_REP, B_HEAD_DIM), q_gain)


def swa_prompt(xn, kv, w_q, q_gain, sinks, w_out, slopes):
    q = swa_query(xn, w_q, q_gain)
    o = banded_attention(q, kv[:, :, 0], kv[:, :, 1], B_WINDOW, slopes, sinks.reshape(B_KV_HEADS, B_REP))
    return o.reshape(*o.shape[:2], -1) @ w_out


def swa_sample(xn, q_pos, kv_all, k_pos, w_q, q_gain, sinks, w_out, slopes):
    q = swa_query(xn, w_q, q_gain)
    o = window_attention_cached(q, q_pos, kv_all[:, :, 0], kv_all[:, :, 1], k_pos, B_WINDOW, slopes,
                                sinks.reshape(B_KV_HEADS, B_REP))
    return o.reshape(*o.shape[:2], -1) @ w_out


def setup_inputs(seed: int = 0) -> dict:
    key = jax.random.key(seed)
    k = jax.random.split(key, 24)
    f32 = jnp.float32

    def nrm(i, shape, scale=1.0):
        return jax.random.normal(k[i], shape, f32) * scale

    n_pages = PAST_LEN // PAGE_SIZE
    n_used = DEC_BATCH * n_pages
    n_pool = n_used + n_used // 4
    page_table = jax.random.permutation(k[0], n_pool)[:n_used].reshape(DEC_BATCH, n_pages).astype(jnp.int32)
    wa_buf = min(A_WINDOW, PAST_LEN)
    wb_buf = min(B_WINDOW, PAST_LEN)
    hd_a = A_HEADS * A_HEAD_DIM
    hd_b = B_HEADS * B_HEAD_DIM
    return {
        'x_prompt': nrm(1, (BATCH, SEQ, D_MODEL)),
        'x_sample': nrm(2, (DEC_BATCH, DEC_SEQ, D_MODEL)),
        'cache_nsa_kv': nrm(3, (N_A_LAYERS, n_pool, PAGE_SIZE, 4, A_KV_GROUPS, A_HEAD_DIM)),
        'cache_nsa_win': nrm(4, (N_A_LAYERS, DEC_BATCH, wa_buf, 2, A_KV_GROUPS, A_HEAD_DIM)),
        'cache_shared_win': nrm(5, (DEC_BATCH, wb_buf, 2, B_KV_HEADS, B_HEAD_DIM)),
        'page_table': page_table,
        'norm_attn': 1.0 + nrm(6, (DEPTH, D_MODEL), 0.05),
        'norm_mlp': 1.0 + nrm(7, (DEPTH, D_MODEL), 0.05),
        'a_w_in': nrm(8, (N_A_LAYERS, D_MODEL, A_IN), D_MODEL ** -0.5),
        'a_q_gain': 1.0 + nrm(9, (N_A_LAYERS, A_HEAD_DIM), 0.05),
        'a_k_gain': 1.0 + nrm(10, (N_A_LAYERS, 3, A_HEAD_DIM), 0.05),
        'a_cmp_pe': nrm(11, (N_A_LAYERS, 2, CMP_LEN, A_HEAD_DIM), 0.1),
        'a_cmp_w1': nrm(12, (N_A_LAYERS, 2, CMP_LEN, A_HEAD_DIM, A_HEAD_DIM), (CMP_LEN * A_HEAD_DIM) ** -0.5),
        'a_cmp_w2': nrm(13, (N_A_LAYERS, 2, A_HEAD_DIM, A_HEAD_DIM), A_HEAD_DIM ** -0.5),
        'a_w_out': nrm(14, (N_A_LAYERS, hd_a, D_MODEL), hd_a ** -0.5),
        'kv_norm': 1.0 + nrm(15, (D_MODEL,), 0.05),
        'kv_w': nrm(16, (D_MODEL, B_KV_OUT), D_MODEL ** -0.5),
        'kv_k_gain': 1.0 + nrm(17, (B_HEAD_DIM,), 0.05),
        'b_w_q': nrm(18, (N_B_LAYERS, D_MODEL, hd_b), D_MODEL ** -0.5),
        'b_q_gain': 1.0 + nrm(19, (N_B_LAYERS, B_HEAD_DIM), 0.05),
        'b_sinks': nrm(20, (N_B_LAYERS, B_HEADS), 0.5),
        'b_w_out': nrm(21, (N_B_LAYERS, hd_b, D_MODEL), hd_b ** -0.5),
        'mlp_w1': nrm(22, (DEPTH, D_MODEL, D_FF), D_MODEL ** -0.5),
        'mlp_w2': nrm(23, (DEPTH, D_FF, D_MODEL), D_FF ** -0.5),
    }


def reference(x_prompt, x_sample, cache_nsa_kv, cache_nsa_win, cache_shared_win, page_table,
              norm_attn, norm_mlp, a_w_in, a_q_gain, a_k_gain, a_cmp_pe, a_cmp_w1, a_cmp_w2, a_w_out,
              kv_norm, kv_w, kv_k_gain, b_w_q, b_q_gain, b_sinks, b_w_out, mlp_w1, mlp_w2):
    slopes_a = alibi_slopes(A_HEADS).reshape(A_KV_GROUPS, A_REP)
    slopes_b = alibi_slopes(B_HEADS).reshape(B_KV_HEADS, B_REP)
    past_len = page_table.shape[1] * cache_nsa_kv.shape[2]
    seq = x_prompt.shape[1]
    q_pos_s = past_len + jnp.arange(x_sample.shape[1])
    hp, hs = x_prompt, x_sample
    rows_p, rows_s, win_p, win_s = [], [], [], []
    kv_p = kv_s_all = k_pos_s = None
    for layer in range(DEPTH):
        xp = rmsnorm(hp, norm_attn[layer])
        xs = rmsnorm(hs, norm_attn[layer])
        if layer < N_A_LAYERS:
            a = layer
            wa = (a_w_in[a], a_q_gain[a], a_k_gain[a], a_cmp_pe[a], a_cmp_w1[a], a_cmp_w2[a], a_w_out[a])
            yp, rp, wp = nsa_prompt(xp, *wa, slopes_a)
            ys, rs, ws = nsa_sample(xs, cache_nsa_kv[a], page_table, cache_nsa_win[a], *wa, slopes_a)
            rows_p.append(rp)
            rows_s.append(rs)
            win_p.append(wp)
            win_s.append(ws)
        else:
            b = layer - N_A_LAYERS
            yp = swa_prompt(xp, kv_p, b_w_q[b], b_q_gain[b], b_sinks[b], b_w_out[b], slopes_b)
            ys = swa_sample(xs, q_pos_s, kv_s_all, k_pos_s, b_w_q[b], b_q_gain[b], b_sinks[b], b_w_out[b], slopes_b)
        hp = hp + yp
        hs = hs + ys
        hp = hp + sq_relu_mlp(rmsnorm(hp, norm_mlp[layer]), mlp_w1[layer], mlp_w2[layer])
        hs = hs + sq_relu_mlp(rmsnorm(hs, norm_mlp[layer]), mlp_w1[layer], mlp_w2[layer])
        if layer == N_A_LAYERS - 1:
            kv_p = shared_kv(hp, kv_norm, kv_w, kv_k_gain)
            kv_s_all = jnp.concatenate([cache_shared_win, shared_kv(hs, kv_norm, kv_w, kv_k_gain)], axis=1)
            k_pos_s = past_len - cache_shared_win.shape[1] + jnp.arange(kv_s_all.shape[1])
    return (hp, hs, jnp.stack(rows_p), jnp.stack(rows_s), jnp.stack(win_p), jnp.stack(win_s),
            kv_p[:, -min(B_WINDOW, seq):], kv_s_all[:, -cache_shared_win.shape[1]:])
```

```python
import functools
import math

import jax
import jax.numpy as jnp
import numpy as np
from jax import lax
from jax.experimental import pallas as pl
from jax.experimental.pallas import tpu as pltpu

A_HEADS = 16
A_HEAD_DIM = 128
A_KV_GROUPS = 2
A_REP = A_HEADS // A_KV_GROUPS
CMP_LEN = 32
CMP_STRIDE = 16
CMP_RATIO = CMP_LEN // CMP_STRIDE
SEL_BLOCK = 64
SEL_TOPK = 16
N_LOCAL_FORCED = 2
A_WINDOW = 512
B_HEADS = 32
B_HEAD_DIM = 64
B_KV_HEADS = 4
B_REP = B_HEADS // B_KV_HEADS
B_WINDOW = 128
NORM_EPS = 1e-6
NEG_INF = -1e30
FORCE_BONUS = 1e4

LANES = 128
V7X_VMEM_BYTES = 64 * 1024 * 1024

_MXU_DTYPE = jnp.bfloat16
_F32 = jnp.float32


def _cparams(sem, vmem_mb):
    return pltpu.CompilerParams(dimension_semantics=sem, vmem_limit_bytes=vmem_mb * 1024 * 1024)


def _dot(a, b):
    return jnp.dot(a, b, preferred_element_type=_F32)


def _dot_nt(a, b):
    return lax.dot_general(a, b, (((1,), (1,)), ((), ())), preferred_element_type=_F32)


def _dot_exact(a, b):
    bm = b.astype(_MXU_DTYPE)
    hi = a.astype(_MXU_DTYPE)
    r1 = a - hi.astype(_F32)
    mid = r1.astype(_MXU_DTYPE)
    lo = (r1 - mid.astype(_F32)).astype(_MXU_DTYPE)
    return _dot(hi, bm) + _dot(mid, bm) + _dot(lo, bm)


def _idiv(x, n):
    if n & (n - 1) == 0:
        return lax.shift_right_logical(x, jnp.int32(n.bit_length() - 1))
    return x // n


def _pick_tile(m, want):
    t = want
    while m % t:
        t //= 2
    assert t >= 8
    return t


def _alibi_slopes(n_heads):
    return np.asarray(2.0 ** (-8.0 * np.arange(1, n_heads + 1) / n_heads), dtype=np.float32)


def _rms_rows(x, gain):
    ms = jnp.mean(x * x, axis=-1, keepdims=True)
    return x * lax.rsqrt(ms + NORM_EPS) * gain


def _head_norm(y, gain, flag, hd):
    outs = []
    for c in range(y.shape[1] // LANES):
        yc = y[:, c * LANES:(c + 1) * LANES]
        y2 = yc * yc
        if hd == LANES:
            ms = jnp.mean(y2, axis=-1, keepdims=True)
        else:
            row = _idiv(lax.broadcasted_iota(jnp.int32, (LANES, LANES), 0), hd)
            col = _idiv(lax.broadcasted_iota(jnp.int32, (LANES, LANES), 1), hd)
            ms = _dot_exact(y2, (row == col).astype(_F32)) * (1.0 / hd)
        yn = yc * lax.rsqrt(ms + NORM_EPS) * gain[:, c * LANES:(c + 1) * LANES]
        outs.append(jnp.where(flag[:, c * LANES:(c + 1) * LANES] > 0, yn, yc))
    return outs[0] if len(outs) == 1 else jnp.concatenate(outs, axis=1)


def _rms_mm_body(x_ref, gam_ref, w_ref, gain_ref, flag_ref, o_ref, xn_ref, *, epi, hd):
    @pl.when(pl.program_id(1) == 0)
    def _():
        xn_ref[...] = _rms_rows(x_ref[...], gam_ref[...]).astype(xn_ref.dtype)

    y = _dot(xn_ref[...], w_ref[...])
    if epi == "headnorm":
        y = _head_norm(y, gain_ref[...], flag_ref[...], hd)
    elif epi == "sigmoid":
        y = jax.nn.sigmoid(y)
    o_ref[...] = y.astype(o_ref.dtype)


def _rms_mm(x, gamma, w, *, epi="none", gain=None, flag=None, hd=LANES, out_dtype=_F32, tm=512, tn=512):
    m, k = x.shape
    n = w.shape[1]
    tm = _pick_tile(m, tm)
    tn = _pick_tile(n, tn)
    if gain is None:
        gain = jnp.ones((n,), _F32)
        flag = jnp.zeros((n,), _F32)
    return pl.pallas_call(
        functools.partial(_rms_mm_body, epi=epi, hd=hd),
        grid=(m // tm, n // tn),
        in_specs=[
            pl.BlockSpec((tm, k), lambda i, j: (i, 0)),
            pl.BlockSpec((1, k), lambda i, j: (0, 0)),
            pl.BlockSpec((k, tn), lambda i, j: (0, j)),
            pl.BlockSpec((1, tn), lambda i, j: (0, j)),
            pl.BlockSpec((1, tn), lambda i, j: (0, j)),
        ],
        out_specs=pl.BlockSpec((tm, tn), lambda i, j: (i, j)),
        out_shape=jax.ShapeDtypeStruct((m, n), out_dtype),
        scratch_shapes=[pltpu.VMEM((tm, k), _MXU_DTYPE)],
        compiler_params=_cparams(("arbitrary", "arbitrary"), 40),
    )(x, gamma.reshape(1, k), w, gain.reshape(1, n), flag.reshape(1, n))


def _mm_res_body(a_ref, w_ref, h_ref, o_ref):
    o_ref[...] = h_ref[...] + _dot(a_ref[...], w_ref[...])


def _mm_res(a, w, h, *, tm=512, tn=512):
    m, k = a.shape
    n = w.shape[1]
    tm = _pick_tile(m, tm)
    tn = _pick_tile(n, tn)
    return pl.pallas_call(
        _mm_res_body,
        grid=(m // tm, n // tn),
        in_specs=[
            pl.BlockSpec((tm, k), lambda i, j: (i, 0)),
            pl.BlockSpec((k, tn), lambda i, j: (0, j)),
            pl.BlockSpec((tm, tn), lambda i, j: (i, j)),
        ],
        out_specs=pl.BlockSpec((tm, tn), lambda i, j: (i, j)),
        out_shape=jax.ShapeDtypeStruct((m, n), _F32),
        compiler_params=_cparams(("arbitrary", "arbitrary"), 40),
    )(a, w, h)


def _mlp_body(h_ref, gam_ref, w1_ref, w2_ref, o_ref, xn_ref):
    @pl.when(pl.program_id(1) == 0)
    def _():
        h = h_ref[...]
        xn_ref[...] = _rms_rows(h, gam_ref[...]).astype(xn_ref.dtype)
        o_ref[...] = h

    a = jnp.square(jnp.maximum(_dot(xn_ref[...], w1_ref[...]), 0.0))
    o_ref[...] += _dot(a.astype(_MXU_DTYPE), w2_ref[...])


def _mlp(h, gamma, w1, w2, *, tm=512, tf=512):
    m, d = h.shape
    f = w1.shape[1]
    tm = _pick_tile(m, tm)
    tf = _pick_tile(f, tf)
    return pl.pallas_call(
        _mlp_body,
        grid=(m // tm, f // tf),
        in_specs=[
            pl.BlockSpec((tm, d), lambda i, j: (i, 0)),
            pl.BlockSpec((1, d), lambda i, j: (0, 0)),
            pl.BlockSpec((d, tf), lambda i, j: (0, j)),
            pl.BlockSpec((tf, d), lambda i, j: (j, 0)),
        ],
        out_specs=pl.BlockSpec((tm, d), lambda i, j: (i, 0)),
        out_shape=jax.ShapeDtypeStruct((m, d), _F32),
        scratch_shapes=[pltpu.VMEM((tm, d), _MXU_DTYPE)],
        compiler_params=_cparams(("arbitrary", "arbitrary"), 48),
    )(h, gamma.reshape(1, d), w1, w2)


def _softmax_parts(s):
    m = jnp.max(s, axis=-1, keepdims=True)
    e = jnp.exp(s - m)
    return e, jnp.sum(e, axis=-1, keepdims=True)


def _topk_mask(score, k, n_cand):
    lane = lax.broadcasted_iota(jnp.int32, score.shape, 1).astype(_F32)
    s = jnp.where(lane < n_cand, score, -jnp.inf)
    sel = jnp.zeros(score.shape, _F32)
    for _ in range(k):
        m = jnp.max(s, axis=-1, keepdims=True)
        idx = jnp.min(jnp.where(s == m, lane, float(4 * score.shape[1])), axis=-1, keepdims=True)
        hit = lane == idx
        sel = jnp.where(hit, 1.0, sel)
        s = jnp.where(hit, -jnp.inf, s)
    return sel


def _select_blocks(imp, qpos, n_sel):
    blk = lax.broadcasted_iota(jnp.int32, imp.shape, 1)
    cur = _idiv(qpos, SEL_BLOCK)
    valid = blk <= cur
    forced = (blk == 0) | (((cur - blk) < N_LOCAL_FORCED) & valid)
    score = jnp.where(valid, imp + FORCE_BONUS * forced.astype(_F32), NEG_INF)
    return _topk_mask(score, min(SEL_TOPK, n_sel), n_sel)


def _compress_x2(x2, w1cat, pe2, w2, n_chunk):
    d = A_HEAD_DIM
    h = _dot(x2.astype(_MXU_DTYPE), w1cat)
    c = _dot(pe2, w1cat)
    part0 = h[:, :d] + c[0:1, :d]
    part1 = h[:, d:] + c[1:2, d:]
    pieces = []
    for s in range(x2.shape[0] // n_chunk):
        p1 = part1[s * n_chunk:(s + 1) * n_chunk]
        pieces.append(part0[s * n_chunk:(s + 1) * n_chunk] + pltpu.roll(p1, n_chunk - 1, 0))
    hh = pieces[0] if len(pieces) == 1 else jnp.concatenate(pieces, axis=0)
    return _dot(jax.nn.gelu(hh).astype(_MXU_DTYPE), w2)


def _cmp_prompt_body(kraw_ref, vraw_ref, w1_ref, pe_ref, w2_ref, kg_ref, kc_ref, vc_ref, *, n_chunk):
    for slot, (src, dst) in enumerate(((kraw_ref, kc_ref), (vraw_ref, vc_ref))):
        x2 = jnp.concatenate([src[pl.ds(s, n_chunk, stride=CMP_STRIDE), :] for s in range(CMP_STRIDE)], axis=1)
        y = _compress_x2(x2, w1_ref[slot], pe_ref[slot], w2_ref[slot], n_chunk)
        if slot == 0:
            y = _rms_rows(y, kg_ref[...])
        dst[...] = y


def _cmp_prompt(kv, w1cat, pe2, w2, kgain0, *, bsz, t_len):
    n_chunk = t_len // CMP_STRIDE
    d = A_HEAD_DIM
    g = A_KV_GROUPS
    out = jax.ShapeDtypeStruct((bsz, g, n_chunk, d), _F32)
    ospec = pl.BlockSpec((None, None, n_chunk, d), lambda b, gi: (b, gi, 0, 0))
    return pl.pallas_call(
        functools.partial(_cmp_prompt_body, n_chunk=n_chunk),
        grid=(bsz, g),
        in_specs=[
            pl.BlockSpec((t_len, d), lambda b, gi: (b, gi)),
            pl.BlockSpec((t_len, d), lambda b, gi: (b, g + gi)),
            pl.BlockSpec(w1cat.shape, lambda b, gi: (0, 0, 0)),
            pl.BlockSpec(pe2.shape, lambda b, gi: (0, 0, 0)),
            pl.BlockSpec(w2.shape, lambda b, gi: (0, 0, 0)),
            pl.BlockSpec((1, d), lambda b, gi: (0, 0)),
        ],
        out_specs=[ospec, ospec],
        out_shape=[out, out],
        compiler_params=_cparams(("arbitrary", "arbitrary"), 32),
    )(kv, kv, w1cat, pe2, w2, kgain0)


def _nsa_prompt_body(slope_ref, q_ref, gt_ref, kc_ref, vc_ref, ks_ref, vs_ref, kw_ref, vw_ref, ov_ref, e_ref,
                     o_ref, ksb, vsb, kwb, vwb, mexp, *, tq, t_len, n_cmp, n_sel, chunk):
    gi = pl.program_id(1)
    i = pl.program_id(2)
    d = A_HEAD_DIM
    rep = A_REP
    scale = d ** -0.5

    @pl.when(i == 0)
    def _():
        ksb[...] = ks_ref[...].astype(ksb.dtype)
        vsb[...] = vs_ref[...].astype(vsb.dtype)
        kwb[...] = kw_ref[...].astype(kwb.dtype)
        vwb[...] = vw_ref[...].astype(vwb.dtype)

    t0 = i * tq
    q = q_ref[...]
    qst = jnp.concatenate([q[:, r * d:(r + 1) * d] for r in range(rep)], axis=0)
    tpos = t0 + lax.broadcasted_iota(jnp.int32, (tq, 1), 0)

    def biased(s, valid, dist):
        outs = [jnp.where(valid, s[r * tq:(r + 1) * tq] * scale - slope_ref[gi * rep + r] * dist, NEG_INF)
                for r in range(rep)]
        return jnp.concatenate(outs, axis=0)

    n_c = kc_ref.shape[0]
    cidx = lax.broadcasted_iota(jnp.int32, (1, n_c), 1)
    cend = cidx * CMP_STRIDE + (CMP_LEN - 1)
    valid_c = (cend <= tpos) & (cidx < n_cmp)
    s = biased(_dot_nt(qst, kc_ref[...].astype(_MXU_DTYPE)), valid_c, (tpos - cend).astype(_F32))
    e, l = _softmax_parts(s)
    any_valid = jnp.concatenate([(tpos >= CMP_LEN - 1).astype(_F32)] * rep, axis=0)
    p_c = e / l * any_valid
    o_cmp = _dot(p_c.astype(_MXU_DTYPE), vc_ref[...].astype(_MXU_DTYPE))
    p_sum = p_c[0:tq]
    for r in range(1, rep):
        p_sum = p_sum + p_c[r * tq:(r + 1) * tq]
    imp = _dot_exact(p_sum, ov_ref[...])
    selm = _select_blocks(imp, tpos, n_sel).astype(_MXU_DTYPE)

    n_chunks_total = t_len // chunk
    for c in range(n_chunks_total):
        mexp[c] = _dot(selm, e_ref[c])
    n_live = (t0 + tq + chunk - 1) // chunk

    def sel_step(c, carry):
        m_i, l_i, acc = carry
        base = pl.multiple_of(c * chunk, chunk)
        kpos = base + lax.broadcasted_iota(jnp.int32, (1, chunk), 1)
        valid = (mexp[c] > 0.5) & (kpos <= tpos)
        s = biased(_dot_nt(qst, ksb[pl.ds(base, chunk), :]), valid, (tpos - kpos).astype(_F32))
        m_new = jnp.maximum(m_i, jnp.max(s, axis=-1, keepdims=True))
        alpha = jnp.exp(m_i - m_new)
        e = jnp.exp(s - m_new)
        l_new = alpha * l_i + jnp.sum(e, axis=-1, keepdims=True)
        acc = alpha * acc + _dot(e.astype(_MXU_DTYPE), vsb[pl.ds(base, chunk), :])
        return m_new, l_new, acc

    init = (jnp.full((rep * tq, 1), NEG_INF, _F32), jnp.zeros((rep * tq, 1), _F32), jnp.zeros((rep * tq, d), _F32))
    _, l_s, acc_s = lax.fori_loop(0, n_live, sel_step, init)
    o_sel = acc_s / l_s

    wlen = A_WINDOW + tq
    start = pl.multiple_of(jnp.maximum(t0 - A_WINDOW, 0), tq)
    kpos = start + lax.broadcasted_iota(jnp.int32, (1, wlen), 1)
    dw = tpos - kpos
    s = biased(_dot_nt(qst, kwb[pl.ds(start, wlen), :]), (dw >= 0) & (dw < A_WINDOW), dw.astype(_F32))
    e, l = _softmax_parts(s)
    o_win = _dot(e.astype(_MXU_DTYPE), vwb[pl.ds(start, wlen), :]) / l

    gt = gt_ref[...]
    for r in range(rep):
        rows = slice(r * tq, (r + 1) * tq)
        o_r = (gt[:, r:r + 1] * o_cmp[rows] + gt[:, rep + r:rep + r + 1] * o_sel[rows]
               + gt[:, 2 * rep + r:2 * rep + r + 1] * o_win[rows])
        o_ref[:, r * d:(r + 1) * d] = o_r.astype(o_ref.dtype)


def _nsa_prompt(q_all, kv_all, gates_r, kc, vc, ov, emat, slopes, *, bsz, t_len, m_total, tq=128, chunk=512):
    d = A_HEAD_DIM
    g = A_KV_GROUPS
    rep = A_REP
    nq = t_len // tq
    n_chunk = kc.shape[2]
    n_cmp = n_chunk - CMP_RATIO + 1
    n_sel = -(-t_len // SEL_BLOCK)
    assert t_len >= A_WINDOW + tq and t_len % chunk == 0 and chunk % SEL_BLOCK == 0 and n_sel <= LANES
    kvspec = lambda col: pl.BlockSpec((t_len, d), lambda b, gi, i, col=col: (b, col + gi))
    return pl.pallas_call(
        functools.partial(_nsa_prompt_body, tq=tq, t_len=t_len, n_cmp=n_cmp, n_sel=n_sel, chunk=chunk),
        grid=(bsz, g, nq),
        in_specs=[
            pl.BlockSpec(memory_space=pltpu.SMEM),
            pl.BlockSpec((tq, rep * d), lambda b, gi, i: (b * nq + i, gi)),
            pl.BlockSpec((None, None, tq, 3 * rep), lambda b, gi, i: (b, gi, i, 0)),
            pl.BlockSpec((None, None, n_chunk, d), lambda b, gi, i: (b, gi, 0, 0)),
            pl.BlockSpec((None, None, n_chunk, d), lambda b, gi, i: (b, gi, 0, 0)),
            kvspec(2 * g), kvspec(3 * g), kvspec(4 * g), kvspec(5 * g),
            pl.BlockSpec(ov.shape, lambda b, gi, i: (0, 0)),
            pl.BlockSpec(emat.shape, lambda b, gi, i: (0, 0, 0)),
        ],
        out_specs=pl.BlockSpec((tq, rep * d), lambda b, gi, i: (b * nq + i, gi)),
        out_shape=jax.ShapeDtypeStruct((m_total, g * rep * d), _MXU_DTYPE),
        scratch_shapes=[pltpu.VMEM((t_len, d), _MXU_DTYPE)] * 4 + [pltpu.VMEM((t_len // chunk, tq, chunk), _F32)],
        compiler_params=_cparams(("arbitrary", "arbitrary", "arbitrary"), 48),
    )(slopes, q_all, gates_r, kc, vc, kv_all, kv_all, kv_all, kv_all, ov, emat)


def _nsa_sample_body(pt_ref, slope_ref, *refs, pps, past, t_new, n_cmp, n_sel):
    pages = refs[:pps]
    (q_ref, gt_ref, ksn_ref, vsn_ref, kwn_ref, vwn_ref, cwin_ref, w1_ref, pe_ref, w2_ref, kg_ref, ov_ref,
     o_ref, x2k, x2v, ksb, vsb, emat) = refs[pps:]
    b = pl.program_id(0)
    pp = pl.program_id(1)
    n_pp = pl.num_programs(1)
    d = A_HEAD_DIM
    g = A_KV_GROUPS
    rep = A_REP
    rows = t_new * rep
    scale = d ** -0.5
    page_len = pages[0].shape[0] // (4 * g)
    cpp = page_len // CMP_STRIDE
    n_chunk = past // CMP_STRIDE
    n_blk_l = emat.shape[0]

    @pl.when((b == 0) & (pp == 0))
    def _():
        blk = lax.broadcasted_iota(jnp.int32, emat.shape, 0)
        key = _idiv(lax.broadcasted_iota(jnp.int32, emat.shape, 1), SEL_BLOCK)
        emat[...] = (blk == key).astype(emat.dtype)

    sg_stride = 4 * g
    for pi in range(pps):
        pg = pages[pi]
        page_no = pp * pps + pi
        for gi in range(g):
            crow = pl.multiple_of(gi * n_chunk + page_no * cpp, cpp)
            for s in range(CMP_STRIDE):
                x2k[pl.ds(crow, cpp), s * d:(s + 1) * d] = pg[pl.ds(s * sg_stride + gi, cpp, stride=CMP_STRIDE * sg_stride), :]
                x2v[pl.ds(crow, cpp), s * d:(s + 1) * d] = pg[pl.ds(s * sg_stride + g + gi, cpp, stride=CMP_STRIDE * sg_stride), :]
            krow = pl.multiple_of(page_no * page_len, page_len)
            ksb[gi, pl.ds(krow, page_len), :] = pg[pl.ds(2 * g + gi, page_len, stride=sg_stride), :].astype(ksb.dtype)
            vsb[gi, pl.ds(krow, page_len), :] = pg[pl.ds(3 * g + gi, page_len, stride=sg_stride), :].astype(vsb.dtype)

    @pl.when(pp == n_pp - 1)
    def _():
        kc_all = _rms_rows(_compress_x2(x2k[...], w1_ref[0], pe_ref[0], w2_ref[0], n_chunk), kg_ref[...])
        vc_all = _compress_x2(x2v[...], w1_ref[1], pe_ref[1], w2_ref[1], n_chunk)
        rowi = lax.broadcasted_iota(jnp.int32, (rows, 1), 0)
        qpos = past + _idiv(rowi, rep)
        head_r = rowi - _idiv(rowi, rep) * rep
        pad_new = jnp.zeros((LANES - ksn_ref.shape[1], d), _F32)

        def padded(ref, gi):
            return jnp.concatenate([ref[gi], pad_new], axis=0).astype(_MXU_DTYPE)

        for gi in range(g):
            q = q_ref[gi].astype(_MXU_DTYPE)
            slope = jnp.zeros((rows, 1), _F32)
            for r in range(rep):
                slope = jnp.where(head_r == r, slope_ref[gi * rep + r], slope)

            def biased(s, valid, dist):
                return jnp.where(valid, s * scale - slope * dist, NEG_INF)

            kc = kc_all[gi * n_chunk:(gi + 1) * n_chunk].astype(_MXU_DTYPE)
            vc = vc_all[gi * n_chunk:(gi + 1) * n_chunk].astype(_MXU_DTYPE)
            cidx = lax.broadcasted_iota(jnp.int32, (1, n_chunk), 1)
            cend = cidx * CMP_STRIDE + (CMP_LEN - 1)
            s = biased(_dot_nt(q, kc), (cend <= qpos) & (cidx < n_cmp), (qpos - cend).astype(_F32))
            e, l = _softmax_parts(s)
            p_c = e / l * (qpos >= CMP_LEN - 1).astype(_F32)
            o_cmp = _dot(p_c.astype(_MXU_DTYPE), vc)
            p3 = p_c.reshape(t_new, rep, n_chunk)
            p_sum = jnp.broadcast_to(jnp.sum(p3, axis=1, keepdims=True), p3.shape).reshape(rows, n_chunk)
            imp = _dot_exact(p_sum, ov_ref[...])
            selm = _select_blocks(imp, qpos, n_sel)

            kpos = lax.broadcasted_iota(jnp.int32, (1, past), 1)
            mexp = _dot(selm.astype(_MXU_DTYPE), emat[...])
            s_p = biased(_dot_nt(q, ksb[gi]), (mexp > 0.5) & (kpos <= qpos), (qpos - kpos).astype(_F32))
            npos = past + lax.broadcasted_iota(jnp.int32, (1, LANES), 1)
            new_blk = past // SEL_BLOCK
            sel_new = selm[:, new_blk:new_blk + 1] > 0.5
            s_n = biased(_dot_nt(q, padded(ksn_ref, gi)), sel_new & (npos <= qpos), (qpos - npos).astype(_F32))
            m = jnp.maximum(jnp.max(s_p, axis=-1, keepdims=True), jnp.max(s_n, axis=-1, keepdims=True))
            e_p = jnp.exp(s_p - m)
            e_n = jnp.exp(s_n - m)
            l = jnp.sum(e_p, axis=-1, keepdims=True) + jnp.sum(e_n, axis=-1, keepdims=True)
            o_sel = (_dot(e_p.astype(_MXU_DTYPE), vsb[gi]) + _dot(e_n.astype(_MXU_DTYPE), padded(vsn_ref, gi))) / l

            wbuf = cwin_ref.shape[0] // (2 * g)
            kw = cwin_ref[pl.ds(gi, wbuf, stride=2 * g), :].astype(_MXU_DTYPE)
            vw = cwin_ref[pl.ds(g + gi, wbuf, stride=2 * g), :].astype(_MXU_DTYPE)
            wpos = past - wbuf + lax.broadcasted_iota(jnp.int32, (1, wbuf), 1)
            dw = qpos - wpos
            s_p = biased(_dot_nt(q, kw), (dw >= 0) & (dw < A_WINDOW), dw.astype(_F32))
            dn = qpos - npos
            s_n = biased(_dot_nt(q, padded(kwn_ref, gi)), (dn >= 0) & (dn < A_WINDOW), dn.astype(_F32))
            m = jnp.maximum(jnp.max(s_p, axis=-1, keepdims=True), jnp.max(s_n, axis=-1, keepdims=True))
            e_p = jnp.exp(s_p - m)
            e_n = jnp.exp(s_n - m)
            l = jnp.sum(e_p, axis=-1, keepdims=True) + jnp.sum(e_n, axis=-1, keepdims=True)
            o_win = (_dot(e_p.astype(_MXU_DTYPE), vw) + _dot(e_n.astype(_MXU_DTYPE), padded(vwn_ref, gi))) / l

            gt = gt_ref[gi]
            o_ref[gi] = gt[:, 0:1] * o_cmp + gt[:, 1:2] * o_sel + gt[:, 2:3] * o_win


def _nsa_sample(page_table, slopes, cache_rows, q_s, gt_s, ksn, vsn, kwn, vwn, cwin, w1cat, pe2, w2, kgain0, ov_s,
                *, past, t_new, page_len, pps=8):
    dbsz, n_pages = page_table.shape
    d = A_HEAD_DIM
    g = A_KV_GROUPS
    rows = t_new * A_REP
    n_chunk = past // CMP_STRIDE
    n_cmp = n_chunk - CMP_RATIO + 1
    n_sel = -(-(past + t_new) // SEL_BLOCK)
    n_blk_l = ov_s.shape[1]
    assert past % CMP_STRIDE == 0 and t_new < CMP_STRIDE and n_pages % pps == 0 and past % SEL_BLOCK == 0
    assert t_new <= SEL_BLOCK and n_sel <= n_blk_l
    page_rows = page_len * 4 * g

    def page_spec(pi):
        return pl.BlockSpec((None, page_rows, d), lambda b, pp, pt, pi=pi: (pt[b * n_pages + pp * pps + pi], 0, 0))

    per_b = lambda shape: pl.BlockSpec((None,) + shape, lambda b, pp, pt: (b,) + (0,) * len(shape))
    whole = lambda a: pl.BlockSpec(a.shape, lambda b, pp, pt: (0,) * a.ndim)
    grid_spec = pltpu.PrefetchScalarGridSpec(
        num_scalar_prefetch=1,
        grid=(dbsz, n_pages // pps),
        in_specs=[pl.BlockSpec(memory_space=pltpu.SMEM)] + [page_spec(pi) for pi in range(pps)] + [
            per_b((g, rows, d)), per_b((g, rows, 3)),
            per_b(ksn.shape[1:]), per_b(vsn.shape[1:]), per_b(kwn.shape[1:]), per_b(vwn.shape[1:]),
            per_b(cwin.shape[1:]),
            whole(w1cat), whole(pe2), whole(w2), whole(kgain0), whole(ov_s),
        ],
        out_specs=per_b((g, rows, d)),
        scratch_shapes=[
            pltpu.VMEM((g * n_chunk, CMP_STRIDE * d), _F32), pltpu.VMEM((g * n_chunk, CMP_STRIDE * d), _F32),
            pltpu.VMEM((g, past, d), _MXU_DTYPE), pltpu.VMEM((g, past, d), _MXU_DTYPE),
            pltpu.VMEM((n_blk_l, past), _MXU_DTYPE),
        ],
    )
    return pl.pallas_call(
        functools.partial(_nsa_sample_body, pps=pps, past=past, t_new=t_new, n_cmp=n_cmp, n_sel=n_sel),
        grid_spec=grid_spec,
        out_shape=jax.ShapeDtypeStruct((dbsz, g, rows, d), _F32),
        compiler_params=_cparams(("arbitrary", "arbitrary"), 60),
    )(page_table.reshape(-1), slopes, *([cache_rows] * pps), q_s, gt_s, ksn, vsn, kwn, vwn, cwin,
      w1cat, pe2, w2, kgain0, ov_s)


def _pair_kv(blk, half):
    lane = lax.broadcasted_iota(jnp.int32, blk.shape, 1)
    other = pltpu.roll(blk, B_HEAD_DIM, 1)
    lo_src, hi_src = (blk, other) if half == 0 else (other, blk)
    lo = jnp.where(lane < B_HEAD_DIM, lo_src, 0.0)
    hi = jnp.where(lane >= B_HEAD_DIM, hi_src, 0.0)
    return jnp.concatenate([lo, hi], axis=0).astype(_MXU_DTYPE)


def _swa_tile(q, kblk, vblk, qpos, kpos, sink_ref, o_write):
    tk = kblk.shape[0]
    slopes = _alibi_slopes(B_HEADS)
    scale = B_HEAD_DIM ** -0.5
    dist_i = qpos - kpos
    valid = (dist_i >= 0) & (dist_i < B_WINDOW)
    dist = dist_i.astype(_F32)
    for kh in range(B_KV_HEADS):
        pb = (kh // 2) * LANES
        kd = _pair_kv(kblk[:, pb:pb + LANES], kh % 2)
        vd = _pair_kv(vblk[:, pb:pb + LANES], kh % 2)
        for p in range(B_REP // 2):
            col0 = (kh * B_REP + 2 * p) * B_HEAD_DIM
            s2 = _dot_nt(q[:, col0:col0 + LANES].astype(_MXU_DTYPE), kd)
            probs = []
            for half in range(2):
                h = kh * B_REP + 2 * p + half
                s = jnp.where(valid, s2[:, half * tk:(half + 1) * tk] * scale - float(slopes[h]) * dist, NEG_INF)
                sink = sink_ref[h]
                m = jnp.maximum(jnp.max(s, axis=-1, keepdims=True), sink)
                e = jnp.exp(s - m)
                probs.append(e / (jnp.sum(e, axis=-1, keepdims=True) + jnp.exp(sink - m)))
            o_write(col0, _dot(jnp.concatenate(probs, axis=1).astype(_MXU_DTYPE), vd))


def _swa_prompt_body(sink_ref, q_ref, kv_ref, o_ref, *, tq):
    i = pl.program_id(1)
    t0 = i * tq
    wlen = B_WINDOW + tq
    start = pl.multiple_of(jnp.maximum(t0 - B_WINDOW, 0), tq)
    kvw = kv_ref[pl.ds(start, wlen), :]
    nk = B_KV_HEADS * B_HEAD_DIM
    qpos = t0 + lax.broadcasted_iota(jnp.int32, (tq, 1), 0)
    kpos = start + lax.broadcasted_iota(jnp.int32, (1, wlen), 1)

    def o_write(col0, val):
        o_ref[:, col0:col0 + LANES] = val.astype(o_ref.dtype)

    _swa_tile(q_ref[...], kvw[:, :nk], kvw[:, nk:], qpos, kpos, sink_ref, o_write)


def _swa_prompt(q_all, kv_all, sinks, *, bsz, t_len, m_total, tq=128):
    nq = t_len // tq
    hd = B_HEADS * B_HEAD_DIM
    assert t_len >= B_WINDOW + tq
    return pl.pallas_call(
        functools.partial(_swa_prompt_body, tq=tq),
        grid=(bsz, nq),
        in_specs=[
            pl.BlockSpec(memory_space=pltpu.SMEM),
            pl.BlockSpec((tq, hd), lambda b, i: (b * nq + i, 0)),
            pl.BlockSpec((t_len, kv_all.shape[1]), lambda b, i: (b, 0)),
        ],
        out_specs=pl.BlockSpec((tq, hd), lambda b, i: (b * nq + i, 0)),
        out_shape=jax.ShapeDtypeStruct((m_total, hd), _MXU_DTYPE),
        compiler_params=_cparams(("arbitrary", "arbitrary"), 40),
    )(sinks, q_all, kv_all)


def _swa_sample_body(sink_ref, q_ref, ckv_ref, nkv_ref, o_ref, *, bt, past, wbuf):
    nk = B_KV_HEADS * B_HEAD_DIM
    rq = q_ref.shape[1]
    rn = nkv_ref.shape[1]
    qpos = past + lax.broadcasted_iota(jnp.int32, (rq, 1), 0)
    kpos = past - wbuf + lax.broadcasted_iota(jnp.int32, (1, wbuf + LANES), 1)
    pad = jnp.zeros((LANES - rn, 2 * nk), _F32)
    for bi in range(bt):
        kv = jnp.concatenate([ckv_ref[bi], nkv_ref[bi], pad], axis=0)

        def o_write(col0, val, bi=bi):
            o_ref[bi, :, col0:col0 + LANES] = val

        _swa_tile(q_ref[bi], kv[:, :nk], kv[:, nk:], qpos, kpos, sink_ref, o_write)


def _swa_sample(q_s, ckv, nkv, sinks, *, past, bt=8):
    dbsz, rq, hd = q_s.shape
    wbuf = ckv.shape[1]
    return pl.pallas_call(
        functools.partial(_swa_sample_body, bt=bt, past=past, wbuf=wbuf),
        grid=(dbsz // bt,),
        in_specs=[
            pl.BlockSpec(memory_space=pltpu.SMEM),
            pl.BlockSpec((bt, rq, hd), lambda b: (b, 0, 0)),
            pl.BlockSpec((bt,) + ckv.shape[1:], lambda b: (b, 0, 0)),
            pl.BlockSpec((bt,) + nkv.shape[1:], lambda b: (b, 0, 0)),
        ],
        out_specs=pl.BlockSpec((bt, rq, hd), lambda b: (b, 0, 0)),
        out_shape=jax.ShapeDtypeStruct((dbsz, rq, hd), _F32),
        compiler_params=_cparams(("arbitrary",), 40),
    )(sinks, q_s, ckv, nkv)


def _overlap(n_cmp, n_sel, rows, cols):
    start = np.arange(n_cmp) * CMP_STRIDE
    end = start + CMP_LEN - 1
    s0 = np.arange(n_sel) * SEL_BLOCK
    s1 = s0 + SEL_BLOCK - 1
    m = np.zeros((rows, cols), np.float32)
    m[:n_cmp, :n_sel] = (start[:, None] <= s1[None, :]) & (end[:, None] >= s0[None, :])
    return jnp.asarray(m)


def _expand_mat(n_blk_l, t_len, chunk):
    key_blk = np.arange(t_len) // SEL_BLOCK
    m = (np.arange(n_blk_l)[:, None] == key_blk[None, :]).astype(np.float32)
    return jnp.asarray(m.reshape(n_blk_l, t_len // chunk, chunk).transpose(1, 0, 2), dtype=_MXU_DTYPE)


def kernel(x_prompt, x_sample, cache_nsa_kv, cache_nsa_win, cache_shared_win, page_table,
           norm_attn, norm_mlp, a_w_in, a_q_gain, a_k_gain, a_cmp_pe, a_cmp_w1, a_cmp_w2, a_w_out,
           kv_norm, kv_w, kv_k_gain, b_w_q, b_q_gain, b_sinks, b_w_out, mlp_w1, mlp_w2):
    bsz, t_len, dm = x_prompt.shape
    dbsz, t_new, _ = x_sample.shape
    n_a = a_w_in.shape[0]
    depth = norm_attn.shape[0]
    n_pool, page_len = cache_nsa_kv.shape[1:3]
    past = page_table.shape[1] * page_len
    d, g, rep = A_HEAD_DIM, A_KV_GROUPS, A_REP
    hq = A_HEADS * d
    hkv = 6 * g * d
    mp = bsz * t_len
    ms = dbsz * t_new
    m_total = mp + ms
    wdt = _MXU_DTYPE
    slopes_a = jnp.asarray(_alibi_slopes(A_HEADS))

    h = jnp.concatenate([x_prompt.reshape(mp, dm), x_sample.reshape(ms, dm)], axis=0)
    rows_p, rows_s, win_p, win_s = [], [], [], []
    kv_sh = None
    for layer in range(depth):
        if layer < n_a:
            a = layer
            w_in = a_w_in[a]
            q_all = _rms_mm(h, norm_attn[layer], w_in[:, :hq].astype(wdt), epi="headnorm",
                            gain=jnp.tile(a_q_gain[a], A_HEADS), flag=jnp.ones((hq,), _F32), out_dtype=wdt)
            one, zero = jnp.ones((g * d,), _F32), jnp.zeros((g * d,), _F32)
            kgain = a_k_gain[a]
            kv_gain = jnp.concatenate([one, one, jnp.tile(kgain[1], g), one, jnp.tile(kgain[2], g), one])
            kv_flag = jnp.concatenate([zero, zero, one, zero, one, zero])
            kv_all = _rms_mm(h, norm_attn[layer], w_in[:, hq:hq + hkv].astype(wdt), epi="headnorm",
                             gain=kv_gain, flag=kv_flag)
            n_gate = 3 * A_HEADS
            w_gate = jnp.pad(w_in[:, hq + hkv:], ((0, 0), (0, LANES - n_gate))).astype(wdt)
            gates = _rms_mm(h, norm_attn[layer], w_gate, epi="sigmoid")[:, :n_gate]

            w1cat = (a_cmp_w1[a].reshape(2, CMP_RATIO, CMP_STRIDE, d, d).transpose(0, 2, 3, 1, 4)
                     .reshape(2, CMP_STRIDE * d, CMP_RATIO * d).astype(wdt))
            pe2 = jnp.pad(a_cmp_pe[a].reshape(2, CMP_RATIO, CMP_STRIDE * d), ((0, 0), (0, 8 - CMP_RATIO), (0, 0))).astype(wdt)
            w2 = a_cmp_w2[a].astype(wdt)
            kgain0 = kgain[0].reshape(1, d)

            kc, vc = _cmp_prompt(kv_all, w1cat, pe2, w2, kgain0, bsz=bsz, t_len=t_len)
            n_chunk = t_len // CMP_STRIDE
            n_sel_p = -(-t_len // SEL_BLOCK)
            ov_p = _overlap(n_chunk - CMP_RATIO + 1, n_sel_p, n_chunk, LANES)
            chunk = 512
            emat = _expand_mat(LANES, t_len, chunk)
            gates_p = (gates[:mp].reshape(bsz, t_len, 3, g, rep).transpose(0, 3, 1, 2, 4)
                       .reshape(bsz, g, t_len, 3 * rep))
            o_all = _nsa_prompt(q_all, kv_all, gates_p, kc, vc, ov_p, emat, slopes_a,
                                bsz=bsz, t_len=t_len, m_total=m_total, chunk=chunk)

            q_s = (q_all[mp:].astype(_F32).reshape(dbsz, t_new, g, rep, d).transpose(0, 2, 1, 3, 4)
                   .reshape(dbsz, g, t_new * rep, d))
            gt_s = (gates[mp:].reshape(dbsz, t_new, 3, g, rep).transpose(0, 3, 1, 4, 2)
                    .reshape(dbsz, g, t_new * rep, 3))
            kv_s = kv_all[mp:].reshape(dbsz, t_new, 6, g, d)
            new_rows = lambda slot: jnp.pad(kv_s[:, :, slot].transpose(0, 2, 1, 3), ((0, 0), (0, 0), (0, 8 - t_new), (0, 0)))
            cache_rows = cache_nsa_kv[a].reshape(n_pool, page_len * 4 * g, d)
            cwin = cache_nsa_win[a]
            wbuf = cwin.shape[1]
            n_chunk_s = past // CMP_STRIDE
            n_sel_s = -(-(past + t_new) // SEL_BLOCK)
            n_blk_l = -(-n_sel_s // LANES) * LANES
            ov_s = _overlap(n_chunk_s - CMP_RATIO + 1, n_sel_s, n_chunk_s, n_blk_l)
            o_s = _nsa_sample(page_table, slopes_a, cache_rows, q_s, gt_s, new_rows(2), new_rows(3), new_rows(4),
                              new_rows(5), cwin.reshape(dbsz, wbuf * 2 * g, d), w1cat, pe2, w2, kgain0, ov_s,
                              past=past, t_new=t_new, page_len=page_len)
            o_s = o_s.reshape(dbsz, g, t_new, rep, d).transpose(0, 2, 1, 3, 4).reshape(ms, hq).astype(wdt)
            o_all = lax.dynamic_update_slice(o_all, o_s, (mp, 0))
            w_out = a_w_out[a]

            rows_p.append(kv_all[:mp, :4 * g * d].reshape(bsz, t_len, 4, g, d))
            rows_s.append(kv_all[mp:, :4 * g * d].reshape(dbsz, t_new, 4, g, d))
            wp = kv_all[:mp, 4 * g * d:].reshape(bsz, t_len, 2, g, d)
            win_p.append(wp[:, -min(A_WINDOW, t_len):])
            ws = kv_all[mp:, 4 * g * d:].reshape(dbsz, t_new, 2, g, d)
            win_s.append(jnp.concatenate([cwin, ws], axis=1)[:, -wbuf:])
        else:
            bl = layer - n_a
            hb = B_HEADS * B_HEAD_DIM
            q_all = _rms_mm(h, norm_attn[layer], b_w_q[bl].astype(wdt), epi="headnorm", hd=B_HEAD_DIM,
                            gain=jnp.tile(b_q_gain[bl], B_HEADS), flag=jnp.ones((hb,), _F32), out_dtype=wdt)
            o_all = _swa_prompt(q_all, kv_sh, b_sinks[bl], bsz=bsz, t_len=t_len, m_total=m_total)
            q_s = jnp.pad(q_all[mp:].astype(_F32).reshape(dbsz, t_new, hb), ((0, 0), (0, 8 - t_new), (0, 0)))
            nkv = jnp.pad(kv_sh[mp:].reshape(dbsz, t_new, -1), ((0, 0), (0, 8 - t_new), (0, 0)))
            ckv = cache_shared_win.reshape(dbsz, cache_shared_win.shape[1], -1)
            o_s = _swa_sample(q_s, ckv, nkv, b_sinks[bl], past=past)[:, :t_new].reshape(ms, hb).astype(wdt)
            o_all = lax.dynamic_update_slice(o_all, o_s, (mp, 0))
            w_out = b_w_out[bl]

        h = _mm_res(o_all, w_out.astype(wdt), h)
        h = _mlp(h, norm_mlp[layer], mlp_w1[layer].astype(wdt), mlp_w2[layer].astype(wdt))
        if layer == n_a - 1:
            nk = B_KV_HEADS * B_HEAD_DIM
            kv_sh = _rms_mm(h, kv_norm, kv_w.astype(wdt), epi="headnorm", hd=B_HEAD_DIM,
                            gain=jnp.concatenate([jnp.tile(kv_k_gain, B_KV_HEADS), jnp.ones((nk,), _F32)]),
                            flag=jnp.concatenate([jnp.ones((nk,), _F32), jnp.zeros((nk,), _F32)]))

    wb = cache_shared_win.shape[1]
    kv_p = kv_sh[:mp].reshape(bsz, t_len, 2, B_KV_HEADS, B_HEAD_DIM)
    kv_s_new = kv_sh[mp:].reshape(dbsz, t_new, 2, B_KV_HEADS, B_HEAD_DIM)
    return (h[:mp].reshape(bsz, t_len, dm), h[mp:].reshape(dbsz, t_new, dm),
            jnp.stack(rows_p), jnp.stack(rows_s), jnp.stack(win_p), jnp.stack(win_s),
            kv_p[:, -min(B_WINDOW, t_len):], jnp.concatenate([cache_shared_win, kv_s_new], axis=1)[:, -wb:])
```

```python
import functools

import jax
import jax.numpy as jnp
import numpy as np
from jax import lax
from jax.experimental import pallas as pl
from jax.experimental.pallas import tpu as pltpu

A_HEADS = 16
A_HEAD_DIM = 128
A_KV_GROUPS = 2
A_REP = A_HEADS // A_KV_GROUPS
CMP_LEN = 32
CMP_STRIDE = 16
CMP_RATIO = CMP_LEN // CMP_STRIDE
SEL_BLOCK = 64
SEL_TOPK = 16
N_LOCAL_FORCED = 2
A_WINDOW = 512
B_HEADS = 32
B_HEAD_DIM = 64
B_KV_HEADS = 4
B_REP = B_HEADS // B_KV_HEADS
B_WINDOW = 128
NORM_EPS = 1e-6
NEG_INF = -1e30
FORCE_BONUS = 1e4

LANES = 128
SUBLANES = 8

LOG2E = float(np.log2(np.e))
MASKED_DIST = 1e33

_MXU_DTYPE = jnp.bfloat16
_F32 = jnp.float32


def _cparams(sem, vmem_mb):
    return pltpu.CompilerParams(dimension_semantics=sem, vmem_limit_bytes=vmem_mb * 1024 * 1024)


def _dot(a, b):
    return jnp.dot(a, b, preferred_element_type=_F32)


def _dot_nt(a, b):
    return lax.dot_general(a, b, (((1,), (1,)), ((), ())), preferred_element_type=_F32)


def _split3(a):
    hi = a.astype(_MXU_DTYPE)
    r1 = a - hi.astype(_F32)
    mid = r1.astype(_MXU_DTYPE)
    lo = (r1 - mid.astype(_F32)).astype(_MXU_DTYPE)
    return hi, mid, lo


def _dot_exact(a, b01):
    bm = b01.astype(_MXU_DTYPE)
    hi, mid, lo = _split3(a)
    return _dot(hi, bm) + _dot(mid, bm) + _dot(lo, bm)


def _dot_exact_rhs(a01, b):
    am = a01.astype(_MXU_DTYPE)
    hi, mid, lo = _split3(b)
    return _dot(am, hi) + _dot(am, mid) + _dot(am, lo)


def _idiv(x, n):
    if n & (n - 1) == 0:
        return lax.shift_right_logical(x, jnp.int32(n.bit_length() - 1))
    return x // n


def _pick_tile(m, want):
    if m <= want:
        return m
    for t in range(want - want % SUBLANES, 0, -SUBLANES):
        if m % t == 0:
            return t
    raise ValueError(f"no tile for {m}")


def _alibi_slopes(n_heads):
    return np.asarray(2.0 ** (-8.0 * np.arange(1, n_heads + 1) / n_heads), dtype=np.float32)


def _rms_rows(x, gain):
    ms = jnp.mean(x * x, axis=-1, keepdims=True)
    return x * lax.rsqrt(ms + NORM_EPS) * gain


def _head_norm(y, gain, flag, hd):
    outs = []
    for c in range(y.shape[1] // LANES):
        yc = y[:, c * LANES:(c + 1) * LANES]
        y2 = yc * yc
        if hd == LANES:
            ms = jnp.mean(y2, axis=-1, keepdims=True)
        else:
            row = _idiv(lax.broadcasted_iota(jnp.int32, (LANES, LANES), 0), hd)
            col = _idiv(lax.broadcasted_iota(jnp.int32, (LANES, LANES), 1), hd)
            ms = _dot_exact(y2, (row == col).astype(_F32)) * (1.0 / hd)
        yn = yc * lax.rsqrt(ms + NORM_EPS) * gain[:, c * LANES:(c + 1) * LANES]
        outs.append(jnp.where(flag[:, c * LANES:(c + 1) * LANES] > 0, yn, yc))
    return outs[0] if len(outs) == 1 else jnp.concatenate(outs, axis=1)


def _rms_mm_body(x_ref, gam_ref, w_ref, gain_ref, flag_ref, o_ref, xn_ref, *, epi, hd, transpose_out):
    @pl.when(pl.program_id(1) == 0)
    def _():
        xn_ref[...] = _rms_rows(x_ref[...], gam_ref[...]).astype(xn_ref.dtype)

    y = _dot(xn_ref[...], w_ref[...])
    if epi == "headnorm":
        y = _head_norm(y, gain_ref[...], flag_ref[...], hd)
    elif epi == "sigmoid":
        y = jax.nn.sigmoid(y)
    if transpose_out:
        y = y.T
    o_ref[...] = y.astype(o_ref.dtype)


def _rms_mm(x, gamma, w, *, epi="none", gain=None, flag=None, hd=LANES, out_dtype=_F32, transpose_out=False,
            tm=512, tn=2048):
    m, k = x.shape
    n = w.shape[1]
    tm = _pick_tile(m, tm)
    tn = _pick_tile(n, tn)
    if gain is None:
        gain = jnp.ones((n,), _F32)
        flag = jnp.zeros((n,), _F32)
    if transpose_out:
        out_spec = pl.BlockSpec((tn, tm), lambda i, j: (j, i))
        out_shape = jax.ShapeDtypeStruct((n, m), out_dtype)
    else:
        out_spec = pl.BlockSpec((tm, tn), lambda i, j: (i, j))
        out_shape = jax.ShapeDtypeStruct((m, n), out_dtype)
    return pl.pallas_call(
        functools.partial(_rms_mm_body, epi=epi, hd=hd, transpose_out=transpose_out),
        grid=(m // tm, n // tn),
        in_specs=[
            pl.BlockSpec((tm, k), lambda i, j: (i, 0)),
            pl.BlockSpec((1, k), lambda i, j: (0, 0)),
            pl.BlockSpec((k, tn), lambda i, j: (0, j)),
            pl.BlockSpec((1, tn), lambda i, j: (0, j)),
            pl.BlockSpec((1, tn), lambda i, j: (0, j)),
        ],
        out_specs=out_spec,
        out_shape=out_shape,
        scratch_shapes=[pltpu.VMEM((tm, k), _MXU_DTYPE)],
        compiler_params=_cparams(("arbitrary", "arbitrary"), 56),
    )(x, gamma.reshape(1, k), w, gain.reshape(1, n), flag.reshape(1, n))


def _mm_res_body(a_ref, w_ref, h_ref, o_ref):
    o_ref[...] = h_ref[...] + _dot(a_ref[...], w_ref[...])


def _mm_res(a, w, h, *, tm=512, tn=2048):
    m, k = a.shape
    n = w.shape[1]
    tm = _pick_tile(m, tm)
    tn = _pick_tile(n, tn)
    return pl.pallas_call(
        _mm_res_body,
        grid=(m // tm, n // tn),
        in_specs=[
            pl.BlockSpec((tm, k), lambda i, j: (i, 0)),
            pl.BlockSpec((k, tn), lambda i, j: (0, j)),
            pl.BlockSpec((tm, tn), lambda i, j: (i, j)),
        ],
        out_specs=pl.BlockSpec((tm, tn), lambda i, j: (i, j)),
        out_shape=jax.ShapeDtypeStruct((m, n), _F32),
        compiler_params=_cparams(("arbitrary", "arbitrary"), 56),
    )(a, w, h)


def _mlp_body(h_ref, gam_ref, w1_ref, w2_ref, o_ref, xn_ref):
    @pl.when(pl.program_id(1) == 0)
    def _():
        h = h_ref[...]
        xn_ref[...] = _rms_rows(h, gam_ref[...]).astype(xn_ref.dtype)
        o_ref[...] = h

    a = jnp.square(jnp.maximum(_dot(xn_ref[...], w1_ref[...]), 0.0))
    o_ref[...] += _dot(a.astype(_MXU_DTYPE), w2_ref[...])


def _mlp(h, gamma, w1, w2, *, tm=768, tf=512):
    m, d = h.shape
    f = w1.shape[1]
    tm = _pick_tile(m, tm)
    tf = _pick_tile(f, tf)
    return pl.pallas_call(
        _mlp_body,
        grid=(m // tm, f // tf),
        in_specs=[
            pl.BlockSpec((tm, d), lambda i, j: (i, 0)),
            pl.BlockSpec((1, d), lambda i, j: (0, 0)),
            pl.BlockSpec((d, tf), lambda i, j: (0, j)),
            pl.BlockSpec((tf, d), lambda i, j: (j, 0)),
        ],
        out_specs=pl.BlockSpec((tm, d), lambda i, j: (i, 0)),
        out_shape=jax.ShapeDtypeStruct((m, d), _F32),
        scratch_shapes=[pltpu.VMEM((tm, d), _MXU_DTYPE)],
        compiler_params=_cparams(("arbitrary", "arbitrary"), 56),
    )(h, gamma.reshape(1, d), w1, w2)


def _block_scores(imp, blk, cur, n_sel):
    valid = blk <= cur
    forced = (blk == 0) | (((cur - blk) < N_LOCAL_FORCED) & valid)
    score = jnp.where(valid, imp + FORCE_BONUS * forced.astype(_F32), NEG_INF)
    return jnp.where(blk < n_sel, score, -jnp.inf)


def _select_blocks_rows(imp, qpos, n_sel):
    blk = lax.broadcasted_iota(jnp.int32, imp.shape, 1)
    s = _block_scores(imp, blk, _idiv(qpos, SEL_BLOCK), n_sel)
    sel = jnp.zeros(imp.shape, _F32)
    for _ in range(min(SEL_TOPK, n_sel)):
        hit = blk == jnp.argmax(s, axis=-1, keepdims=True).astype(jnp.int32)
        sel = jnp.where(hit, 1.0, sel)
        s = jnp.where(hit, -jnp.inf, s)
    return sel


def _select_blocks_cols(imp_t, qpos, n_sel):
    blk = lax.broadcasted_iota(jnp.int32, imp_t.shape, 0)
    s = _block_scores(imp_t, blk, _idiv(qpos, SEL_BLOCK), n_sel)
    beaten = jnp.zeros(imp_t.shape, _F32)
    for j in range(n_sel):
        sj = s[j:j + 1, :]
        tie = jnp.where(blk > j, 1.0, 0.0)
        beaten = beaten + jnp.where(sj > s, 1.0, 0.0) + jnp.where(sj == s, tie, 0.0)
    return jnp.where((beaten < min(SEL_TOPK, n_sel)) & (blk < n_sel), 1.0, 0.0)


def _compress_x2(x2, w1cat, pe2, w2, n_chunk):
    d = A_HEAD_DIM
    h = _dot(x2.astype(_MXU_DTYPE), w1cat)
    c = _dot(pe2, w1cat)
    part0 = h[:, :d] + c[0:1, :d]
    part1 = h[:, d:] + c[1:2, d:]
    pieces = []
    for s in range(x2.shape[0] // n_chunk):
        p1 = part1[s * n_chunk:(s + 1) * n_chunk]
        pieces.append(part0[s * n_chunk:(s + 1) * n_chunk] + pltpu.roll(p1, n_chunk - 1, 0))
    hh = pieces[0] if len(pieces) == 1 else jnp.concatenate(pieces, axis=0)
    return _dot(jax.nn.gelu(hh).astype(_MXU_DTYPE), w2)


def _softmax_cols(s_t, distm, slope2, scale2, sink2=None):
    sc = s_t * scale2 - slope2 * distm
    m = jnp.max(sc, axis=0, keepdims=True)
    if sink2 is not None:
        m = jnp.maximum(m, sink2)
    e = jnp.exp2(sc - m)
    l = jnp.sum(e, axis=0, keepdims=True)
    if sink2 is not None:
        l = l + jnp.exp2(sink2 - m)
    return e, m, l


def _cmp_prompt_body(kraw_ref, vraw_ref, w1_ref, pe_ref, w2_ref, kg_ref, kc_ref, vct_ref, *, n_chunk):
    for slot, src in enumerate((kraw_ref, vraw_ref)):
        x2 = jnp.concatenate([src[pl.ds(s, n_chunk, stride=CMP_STRIDE), :] for s in range(CMP_STRIDE)], axis=1)
        y = _compress_x2(x2, w1_ref[slot], pe_ref[slot], w2_ref[slot], n_chunk)
        if slot == 0:
            kc_ref[...] = _rms_rows(y, kg_ref[...])
        else:
            vct_ref[...] = y.T


def _cmp_prompt(kv, w1cat, pe2, w2, kgain0, *, bsz, t_len):
    n_chunk = t_len // CMP_STRIDE
    d = A_HEAD_DIM
    g = A_KV_GROUPS
    return pl.pallas_call(
        functools.partial(_cmp_prompt_body, n_chunk=n_chunk),
        grid=(bsz, g),
        in_specs=[
            pl.BlockSpec((t_len, d), lambda b, gi: (b, gi)),
            pl.BlockSpec((t_len, d), lambda b, gi: (b, g + gi)),
            pl.BlockSpec(w1cat.shape, lambda b, gi: (0, 0, 0)),
            pl.BlockSpec(pe2.shape, lambda b, gi: (0, 0, 0)),
            pl.BlockSpec(w2.shape, lambda b, gi: (0, 0, 0)),
            pl.BlockSpec((1, d), lambda b, gi: (0, 0)),
        ],
        out_specs=[pl.BlockSpec((None, None, n_chunk, d), lambda b, gi: (b, gi, 0, 0)),
                   pl.BlockSpec((None, None, d, n_chunk), lambda b, gi: (b, gi, 0, 0))],
        out_shape=[jax.ShapeDtypeStruct((bsz, g, n_chunk, d), _F32), jax.ShapeDtypeStruct((bsz, g, d, n_chunk), _F32)],
        compiler_params=_cparams(("arbitrary", "arbitrary"), 32),
    )(kv, kv, w1cat, pe2, w2, kgain0)


def _nsa_prompt_body(slope_ref, qt_ref, gt_ref, kc_ref, vct_ref, ks_ref, vs_ref, kw_ref, vw_ref, ovt_ref, et_ref,
                     o_ref, ksb, vst, kwb, vwt, *, tq, t_len, n_cmp, n_sel, chunk):
    gi = pl.program_id(1)
    i = pl.program_id(2)
    d = A_HEAD_DIM
    rep = A_REP
    scale2 = d ** -0.5 * LOG2E
    n_chunks = t_len // chunk
    n_wblk = A_WINDOW // tq + 1

    @pl.when(i == 0)
    def _():
        ksb[...] = ks_ref[...].astype(ksb.dtype)
        kwb[...] = kw_ref[...].astype(kwb.dtype)
        for c in range(n_chunks):
            vst[c] = vs_ref[c * chunk:(c + 1) * chunk, :].T.astype(vst.dtype)
        for c in range(t_len // tq):
            vwt[c] = vw_ref[c * tq:(c + 1) * tq, :].T.astype(vwt.dtype)

    t0 = i * tq
    qt = qt_ref[...]
    qst = jnp.concatenate([qt[r * d:(r + 1) * d, :] for r in range(rep)], axis=1)
    tpos = t0 + lax.broadcasted_iota(jnp.int32, (1, tq), 1)
    slope2 = [slope_ref[gi * rep + r] * LOG2E for r in range(rep)]
    head = lambda a, r: a[:, r * tq:(r + 1) * tq]

    def masked_dist(valid, dist):
        return jnp.where(valid, dist.astype(_F32), MASKED_DIST)

    n_c = kc_ref.shape[0]
    cidx = lax.broadcasted_iota(jnp.int32, (n_c, 1), 0)
    cend = cidx * CMP_STRIDE + (CMP_LEN - 1)
    distm = masked_dist((cend <= tpos) & (cidx < n_cmp), tpos - cend)
    s_t = _dot(kc_ref[...].astype(_MXU_DTYPE), qst)
    any_valid = (tpos >= CMP_LEN - 1).astype(_F32)
    p_heads = []
    for r in range(rep):
        e, _, l = _softmax_cols(head(s_t, r), distm, slope2[r], scale2)
        p_heads.append(e * (any_valid / l))
    p_sum = p_heads[0]
    for r in range(1, rep):
        p_sum = p_sum + p_heads[r]
    o_cmp = _dot(vct_ref[...].astype(_MXU_DTYPE), jnp.concatenate(p_heads, axis=1).astype(_MXU_DTYPE))
    imp_t = _dot_exact_rhs(ovt_ref[...], p_sum)
    sel_t = _select_blocks_cols(imp_t, tpos, n_sel).astype(_MXU_DTYPE)

    n_live = (t0 + tq + chunk - 1) // chunk

    def sel_step(c, carry):
        m_i, l_i, acc = carry
        base = pl.multiple_of(c * chunk, chunk)
        kpos = base + lax.broadcasted_iota(jnp.int32, (chunk, 1), 0)
        picked = _dot(et_ref[c], sel_t)
        distm = masked_dist((picked > 0.5) & (kpos <= tpos), tpos - kpos)
        s_t = _dot(ksb[pl.ds(base, chunk), :], qst)
        es, ms, ls = [], [], []
        for r in range(rep):
            sc = head(s_t, r) * scale2 - slope2[r] * distm
            m_new = jnp.maximum(head(m_i, r), jnp.max(sc, axis=0, keepdims=True))
            e = jnp.exp2(sc - m_new)
            es.append(e.astype(_MXU_DTYPE))
            ms.append(m_new)
            ls.append(jnp.sum(e, axis=0, keepdims=True))
        m_new = jnp.concatenate(ms, axis=1)
        alpha = jnp.exp2(m_i - m_new)
        l_new = alpha * l_i + jnp.concatenate(ls, axis=1)
        acc = alpha * acc + _dot(vst[c], jnp.concatenate(es, axis=1))
        return m_new, l_new, acc

    init = (jnp.full((1, rep * tq), NEG_INF, _F32), jnp.zeros((1, rep * tq), _F32), jnp.zeros((d, rep * tq), _F32))
    _, l_s, acc_s = lax.fori_loop(0, n_live, sel_step, init)
    o_sel = acc_s * (1.0 / l_s)

    wlen = n_wblk * tq
    wb0 = jnp.maximum(i - (n_wblk - 1), 0)
    start = pl.multiple_of(wb0 * tq, tq)
    kpos = start + lax.broadcasted_iota(jnp.int32, (wlen, 1), 0)
    dw = tpos - kpos
    distm = masked_dist((dw >= 0) & (dw < A_WINDOW), dw)
    s_t = _dot(kwb[pl.ds(start, wlen), :], qst)
    es, ls = [], []
    for r in range(rep):
        e, _, l = _softmax_cols(head(s_t, r), distm, slope2[r], scale2)
        es.append(e.astype(_MXU_DTYPE))
        ls.append(l)
    vw = jnp.concatenate([vwt[wb0 + j] for j in range(n_wblk)], axis=1)
    o_win = _dot(vw, jnp.concatenate(es, axis=1)) * (1.0 / jnp.concatenate(ls, axis=1))

    gt = gt_ref[...]
    for r in range(rep):
        o_r = (gt[r:r + 1] * head(o_cmp, r) + gt[rep + r:rep + r + 1] * head(o_sel, r)
               + gt[2 * rep + r:2 * rep + r + 1] * head(o_win, r))
        o_ref[:, r * d:(r + 1) * d] = o_r.T.astype(o_ref.dtype)


def _nsa_prompt(qt_all, kv_all, gates_t, kc, vct, ovt, et, slopes, *, bsz, t_len, m_total, tq=256, chunk=512):
    d = A_HEAD_DIM
    g = A_KV_GROUPS
    rep = A_REP
    nq = t_len // tq
    n_chunk = kc.shape[2]
    n_cmp = n_chunk - CMP_RATIO + 1
    n_sel = -(-t_len // SEL_BLOCK)
    assert t_len % tq == 0 and A_WINDOW % tq == 0 and t_len >= A_WINDOW + tq
    assert t_len % chunk == 0 and chunk % SEL_BLOCK == 0
    kvspec = lambda col: pl.BlockSpec((t_len, d), lambda b, gi, i, col=col: (b, col + gi))
    return pl.pallas_call(
        functools.partial(_nsa_prompt_body, tq=tq, t_len=t_len, n_cmp=n_cmp, n_sel=n_sel, chunk=chunk),
        grid=(bsz, g, nq),
        in_specs=[
            pl.BlockSpec(memory_space=pltpu.SMEM),
            pl.BlockSpec((rep * d, tq), lambda b, gi, i: (gi, b * nq + i)),
            pl.BlockSpec((None, None, 3 * rep, tq), lambda b, gi, i: (b, gi, 0, i)),
            pl.BlockSpec((None, None, n_chunk, d), lambda b, gi, i: (b, gi, 0, 0)),
            pl.BlockSpec((None, None, d, n_chunk), lambda b, gi, i: (b, gi, 0, 0)),
            kvspec(2 * g), kvspec(3 * g), kvspec(4 * g), kvspec(5 * g),
            pl.BlockSpec(ovt.shape, lambda b, gi, i: (0, 0)),
            pl.BlockSpec(et.shape, lambda b, gi, i: (0, 0, 0)),
        ],
        out_specs=pl.BlockSpec((tq, rep * d), lambda b, gi, i: (b * nq + i, gi)),
        out_shape=jax.ShapeDtypeStruct((m_total, g * rep * d), _MXU_DTYPE),
        scratch_shapes=[pltpu.VMEM((t_len, d), _MXU_DTYPE), pltpu.VMEM((t_len // chunk, d, chunk), _MXU_DTYPE),
                        pltpu.VMEM((t_len, d), _MXU_DTYPE), pltpu.VMEM((t_len // tq, d, tq), _MXU_DTYPE)],
        compiler_params=_cparams(("arbitrary", "arbitrary", "arbitrary"), 48),
    )(slopes, qt_all, gates_t, kc, vct, kv_all, kv_all, kv_all, kv_all, ovt, et)


def _nsa_sample_body(pt_ref, slope_ref, *refs, pps, past, t_new, n_cmp, n_sel):
    pages = refs[:pps]
    (q_ref, gt_ref, ksn_ref, vsn_ref, kwn_ref, vwn_ref, cwin_ref, w1_ref, pe_ref, w2_ref, kg_ref, ov_ref,
     o_ref, x2k, x2v, ksb, vsb, emat) = refs[pps:]
    b = pl.program_id(0)
    pp = pl.program_id(1)
    n_pp = pl.num_programs(1)
    d = A_HEAD_DIM
    g = A_KV_GROUPS
    rep = A_REP
    rows = t_new * rep
    scale = d ** -0.5
    sg_stride = 4 * g
    page_len = pages[0].shape[0] // sg_stride
    cpp = page_len // CMP_STRIDE
    assert cpp == SUBLANES
    n_chunk = past // CMP_STRIDE
    n_rb = n_chunk // cpp

    @pl.when((b == 0) & (pp == 0))
    def _():
        blk = lax.broadcasted_iota(jnp.int32, emat.shape, 0)
        key = _idiv(lax.broadcasted_iota(jnp.int32, emat.shape, 1), SEL_BLOCK)
        emat[...] = (blk == key).astype(emat.dtype)

    for pi in range(pps):
        pg = pages[pi]
        page_no = pp * pps + pi
        for gi in range(g):
            rb = gi * n_rb + page_no
            for slot, dst in ((0, x2k), (1, x2v)):
                a = pg[pl.ds(slot * g + gi, page_len, stride=sg_stride), :]
                for c in range(cpp):
                    for hs in range(CMP_STRIDE // SUBLANES):
                        p0 = c * CMP_STRIDE + hs * SUBLANES
                        dst[rb, pl.ds(hs * SUBLANES * cpp + c, SUBLANES, stride=cpp), :] = a[p0:p0 + SUBLANES]
            krow = pl.multiple_of(page_no * page_len, page_len)
            ksb[gi, pl.ds(krow, page_len), :] = pg[pl.ds(2 * g + gi, page_len, stride=sg_stride), :].astype(ksb.dtype)
            vsb[gi, pl.ds(krow, page_len), :] = pg[pl.ds(3 * g + gi, page_len, stride=sg_stride), :].astype(vsb.dtype)

    @pl.when(pp == n_pp - 1)
    def _():
        def chunk_rows(src):
            slabs = []
            for rb in range(g * n_rb):
                slab = src[rb]
                slabs.append(jnp.concatenate([slab[s * cpp:(s + 1) * cpp] for s in range(CMP_STRIDE)], axis=1))
            return jnp.concatenate(slabs, axis=0)

        kc_all = _rms_rows(_compress_x2(chunk_rows(x2k), w1_ref[0], pe_ref[0], w2_ref[0], n_chunk), kg_ref[...])
        vc_all = _compress_x2(chunk_rows(x2v), w1_ref[1], pe_ref[1], w2_ref[1], n_chunk)
        rowi = lax.broadcasted_iota(jnp.int32, (rows, 1), 0)
        qpos = past + _idiv(rowi, rep)
        head_r = rowi - _idiv(rowi, rep) * rep
        pad_new = jnp.zeros((LANES - ksn_ref.shape[1], d), _F32)

        def padded(ref, gi):
            return jnp.concatenate([ref[gi], pad_new], axis=0).astype(_MXU_DTYPE)

        def biased(s, valid, dist, slope):
            return jnp.where(valid, s * scale - slope * dist, NEG_INF)

        qs, slopes, o_cmps, imps = [], [], [], []
        cidx = lax.broadcasted_iota(jnp.int32, (1, n_chunk), 1)
        cend = cidx * CMP_STRIDE + (CMP_LEN - 1)
        for gi in range(g):
            q = q_ref[gi].astype(_MXU_DTYPE)
            slope = jnp.zeros((rows, 1), _F32)
            for r in range(rep):
                slope = jnp.where(head_r == r, slope_ref[gi * rep + r], slope)
            kc = kc_all[gi * n_chunk:(gi + 1) * n_chunk].astype(_MXU_DTYPE)
            vc = vc_all[gi * n_chunk:(gi + 1) * n_chunk].astype(_MXU_DTYPE)
            s = biased(_dot_nt(q, kc), (cend <= qpos) & (cidx < n_cmp), (qpos - cend).astype(_F32), slope)
            e = jnp.exp(s - jnp.max(s, axis=-1, keepdims=True))
            p_c = e / jnp.sum(e, axis=-1, keepdims=True) * (qpos >= CMP_LEN - 1).astype(_F32)
            o_cmps.append(_dot(p_c.astype(_MXU_DTYPE), vc))
            p3 = p_c.reshape(t_new, rep, n_chunk)
            p_sum = jnp.broadcast_to(jnp.sum(p3, axis=1, keepdims=True), p3.shape).reshape(rows, n_chunk)
            imps.append(_dot_exact(p_sum, ov_ref[...]))
            qs.append(q)
            slopes.append(slope)
        selm_all = _select_blocks_rows(jnp.concatenate(imps, axis=0), jnp.concatenate([qpos] * g, axis=0), n_sel)

        kpos = lax.broadcasted_iota(jnp.int32, (1, past), 1)
        npos = past + lax.broadcasted_iota(jnp.int32, (1, LANES), 1)
        new_blk = past // SEL_BLOCK
        wbuf = cwin_ref.shape[0] // (2 * g)
        wpos = past - wbuf + lax.broadcasted_iota(jnp.int32, (1, wbuf), 1)
        for gi in range(g):
            q, slope = qs[gi], slopes[gi]
            selm = selm_all[gi * rows:(gi + 1) * rows]

            def two_part(s_p, s_n, v_p, v_n):
                m = jnp.maximum(jnp.max(s_p, axis=-1, keepdims=True), jnp.max(s_n, axis=-1, keepdims=True))
                e_p = jnp.exp(s_p - m)
                e_n = jnp.exp(s_n - m)
                l = jnp.sum(e_p, axis=-1, keepdims=True) + jnp.sum(e_n, axis=-1, keepdims=True)
                return (_dot(e_p.astype(_MXU_DTYPE), v_p) + _dot(e_n.astype(_MXU_DTYPE), v_n)) / l

            picked = _dot(selm.astype(_MXU_DTYPE), emat[...])
            s_p = biased(_dot_nt(q, ksb[gi]), (picked > 0.5) & (kpos <= qpos), (qpos - kpos).astype(_F32), slope)
            sel_new = selm[:, new_blk:new_blk + 1] > 0.5
            s_n = biased(_dot_nt(q, padded(ksn_ref, gi)), sel_new & (npos <= qpos), (qpos - npos).astype(_F32), slope)
            o_sel = two_part(s_p, s_n, vsb[gi], padded(vsn_ref, gi))

            kw = cwin_ref[pl.ds(gi, wbuf, stride=2 * g), :].astype(_MXU_DTYPE)
            vw = cwin_ref[pl.ds(g + gi, wbuf, stride=2 * g), :].astype(_MXU_DTYPE)
            dw = qpos - wpos
            s_p = biased(_dot_nt(q, kw), (dw >= 0) & (dw < A_WINDOW), dw.astype(_F32), slope)
            dn = qpos - npos
            s_n = biased(_dot_nt(q, padded(kwn_ref, gi)), (dn >= 0) & (dn < A_WINDOW), dn.astype(_F32), slope)
            o_win = two_part(s_p, s_n, vw, padded(vwn_ref, gi))

            gt = gt_ref[gi]
            o_ref[gi] = gt[:, 0:1] * o_cmps[gi] + gt[:, 1:2] * o_sel + gt[:, 2:3] * o_win


def _nsa_sample(page_table, slopes, cache_rows, q_s, gt_s, ksn, vsn, kwn, vwn, cwin, w1cat, pe2, w2, kgain0, ov_s,
                *, past, t_new, page_len, pps=8):
    dbsz, n_pages = page_table.shape
    d = A_HEAD_DIM
    g = A_KV_GROUPS
    rows = t_new * A_REP
    n_chunk = past // CMP_STRIDE
    n_cmp = n_chunk - CMP_RATIO + 1
    n_sel = -(-(past + t_new) // SEL_BLOCK)
    n_blk_l = ov_s.shape[1]
    assert past % CMP_STRIDE == 0 and t_new < CMP_STRIDE and n_pages % pps == 0 and past % SEL_BLOCK == 0
    assert t_new <= SEL_BLOCK and n_sel <= n_blk_l
    page_rows = page_len * 4 * g
    cpp = page_len // CMP_STRIDE

    def page_spec(pi):
        return pl.BlockSpec((None, page_rows, d), lambda b, pp, pt, pi=pi: (pt[b * n_pages + pp * pps + pi], 0, 0))

    per_b = lambda shape: pl.BlockSpec((None,) + shape, lambda b, pp, pt: (b,) + (0,) * len(shape))
    whole = lambda a: pl.BlockSpec(a.shape, lambda b, pp, pt: (0,) * a.ndim)
    grid_spec = pltpu.PrefetchScalarGridSpec(
        num_scalar_prefetch=1,
        grid=(dbsz, n_pages // pps),
        in_specs=[pl.BlockSpec(memory_space=pltpu.SMEM)] + [page_spec(pi) for pi in range(pps)] + [
            per_b((g, rows, d)), per_b((g, rows, 3)),
            per_b(ksn.shape[1:]), per_b(vsn.shape[1:]), per_b(kwn.shape[1:]), per_b(vwn.shape[1:]),
            per_b(cwin.shape[1:]),
            whole(w1cat), whole(pe2), whole(w2), whole(kgain0), whole(ov_s),
        ],
        out_specs=per_b((g, rows, d)),
        scratch_shapes=[
            pltpu.VMEM((g * n_pages, CMP_STRIDE * cpp, d), _F32), pltpu.VMEM((g * n_pages, CMP_STRIDE * cpp, d), _F32),
            pltpu.VMEM((g, past, d), _MXU_DTYPE), pltpu.VMEM((g, past, d), _MXU_DTYPE),
            pltpu.VMEM((n_blk_l, past), _MXU_DTYPE),
        ],
    )
    return pl.pallas_call(
        functools.partial(_nsa_sample_body, pps=pps, past=past, t_new=t_new, n_cmp=n_cmp, n_sel=n_sel),
        grid_spec=grid_spec,
        out_shape=jax.ShapeDtypeStruct((dbsz, g, rows, d), _F32),
        compiler_params=_cparams(("arbitrary", "arbitrary"), 60),
    )(page_table.reshape(-1), slopes, *([cache_rows] * pps), q_s, gt_s, ksn, vsn, kwn, vwn, cwin,
      w1cat, pe2, w2, kgain0, ov_s)


def _swa_prompt_body(sink_ref, qt_ref, kv_ref, o_ref, kb, vt, *, tq, t_len):
    i = pl.program_id(1)
    hd = B_HEAD_DIM
    nk = B_KV_HEADS * hd
    scale2 = hd ** -0.5 * LOG2E
    slopes = _alibi_slopes(B_HEADS)
    n_wblk = (B_WINDOW + tq) // LANES
    wlen = n_wblk * LANES

    @pl.when(i == 0)
    def _():
        kb[...] = kv_ref[:, :nk].astype(kb.dtype)
        for c in range(t_len // LANES):
            vt[c] = kv_ref[c * LANES:(c + 1) * LANES, nk:].T.astype(vt.dtype)

    t0 = i * tq
    wb0 = jnp.maximum(i * (tq // LANES) - B_WINDOW // LANES, 0)
    start = pl.multiple_of(wb0 * LANES, LANES)
    tpos = t0 + lax.broadcasted_iota(jnp.int32, (1, tq), 1)
    kpos = start + lax.broadcasted_iota(jnp.int32, (wlen, 1), 0)
    dw = tpos - kpos
    distm = jnp.where((dw >= 0) & (dw < B_WINDOW), dw.astype(_F32), MASKED_DIST)
    kwin = kb[pl.ds(start, wlen), :]
    vwin_t = jnp.concatenate([vt[wb0 + j] for j in range(n_wblk)], axis=1)
    qt = qt_ref[...]
    zeros = jnp.zeros((hd, tq), qt.dtype)
    for kh in range(B_KV_HEADS):
        pb = (kh // 2) * LANES
        blocks = []
        for r in range(B_REP):
            h = kh * B_REP + r
            qh = qt[h * hd:(h + 1) * hd, :]
            blocks.append(jnp.concatenate([qh, zeros] if kh % 2 == 0 else [zeros, qh], axis=0))
        s_t = _dot(kwin[:, pb:pb + LANES], jnp.concatenate(blocks, axis=1))
        ps = []
        for r in range(B_REP):
            h = kh * B_REP + r
            e, _, l = _softmax_cols(s_t[:, r * tq:(r + 1) * tq], distm, float(slopes[h]) * LOG2E, scale2,
                                    sink2=sink_ref[h] * LOG2E)
            ps.append((e * (1.0 / l)).astype(_MXU_DTYPE))
        o_t = _dot(vwin_t[kh * hd:(kh + 1) * hd, :], jnp.concatenate(ps, axis=1))
        for p in range(B_REP // 2):
            pair = jnp.concatenate([o_t[:, (2 * p) * tq:(2 * p + 1) * tq], o_t[:, (2 * p + 1) * tq:(2 * p + 2) * tq]], axis=0)
            col0 = (kh * B_REP + 2 * p) * hd
            o_ref[:, col0:col0 + LANES] = pair.T.astype(o_ref.dtype)


def _swa_prompt(qt_all, kv_all, sinks, *, bsz, t_len, m_total, tq=256):
    nq = t_len // tq
    hq = B_HEADS * B_HEAD_DIM
    nk = B_KV_HEADS * B_HEAD_DIM
    assert t_len % tq == 0 and t_len >= B_WINDOW + tq and tq % LANES == 0 and B_WINDOW % LANES == 0
    return pl.pallas_call(
        functools.partial(_swa_prompt_body, tq=tq, t_len=t_len),
        grid=(bsz, nq),
        in_specs=[
            pl.BlockSpec(memory_space=pltpu.SMEM),
            pl.BlockSpec((hq, tq), lambda b, i: (0, b * nq + i)),
            pl.BlockSpec((t_len, kv_all.shape[1]), lambda b, i: (b, 0)),
        ],
        out_specs=pl.BlockSpec((tq, hq), lambda b, i: (b * nq + i, 0)),
        out_shape=jax.ShapeDtypeStruct((m_total, hq), _MXU_DTYPE),
        scratch_shapes=[pltpu.VMEM((t_len, nk), _MXU_DTYPE), pltpu.VMEM((t_len // LANES, nk, LANES), _MXU_DTYPE)],
        compiler_params=_cparams(("arbitrary", "arbitrary"), 40),
    )(sinks, qt_all, kv_all)


def _swa_sample_body(qbd_ref, ckv_ref, nkv_ref, lane_ref, o_ref, *, bt, past, wbuf, tk):
    hd = B_HEAD_DIM
    nk = B_KV_HEADS * hd
    scale2 = hd ** -0.5 * LOG2E
    lanes = qbd_ref.shape[2]
    slope2 = lane_ref[0:1, :]
    sink2 = lane_ref[1:2, :]
    qpos = past + lane_ref[2:3, :].astype(jnp.int32)
    kpos = past - wbuf + lax.broadcasted_iota(jnp.int32, (tk, 1), 0)
    dw = qpos - kpos
    distm = jnp.where((dw >= 0) & (dw < B_WINDOW), dw.astype(_F32), MASKED_DIST)
    lane_kh = _idiv(lax.broadcasted_iota(jnp.int32, (hd, lanes), 1), lanes // B_KV_HEADS)
    pad = jnp.zeros((tk - wbuf - nkv_ref.shape[1], 2 * nk), _F32)
    for bi in range(bt):
        kv = jnp.concatenate([ckv_ref[bi], nkv_ref[bi], pad], axis=0)
        s_t = _dot(kv[:, :nk].astype(_MXU_DTYPE), qbd_ref[bi])
        e, _, l = _softmax_cols(s_t, distm, slope2, scale2, sink2=sink2)
        p_t = (e * (1.0 / l)).astype(_MXU_DTYPE)
        o_full = _dot(kv[:, nk:].T.astype(_MXU_DTYPE), p_t)
        o = jnp.zeros((hd, lanes), _F32)
        for kh in range(B_KV_HEADS):
            o = jnp.where(lane_kh == kh, o_full[kh * hd:(kh + 1) * hd, :], o)
        o_ref[bi] = o


def _swa_sample(qbd, ckv, nkv, lane_consts, *, past, bt=16):
    dbsz, nk, lanes = qbd.shape
    wbuf = ckv.shape[1]
    tk = -(-(wbuf + nkv.shape[1]) // LANES) * LANES
    bt = _pick_tile(dbsz, bt)
    return pl.pallas_call(
        functools.partial(_swa_sample_body, bt=bt, past=past, wbuf=wbuf, tk=tk),
        grid=(dbsz // bt,),
        in_specs=[
            pl.BlockSpec((bt,) + qbd.shape[1:], lambda b: (b, 0, 0)),
            pl.BlockSpec((bt,) + ckv.shape[1:], lambda b: (b, 0, 0)),
            pl.BlockSpec((bt,) + nkv.shape[1:], lambda b: (b, 0, 0)),
            pl.BlockSpec(lane_consts.shape, lambda b: (0, 0)),
        ],
        out_specs=pl.BlockSpec((bt, B_HEAD_DIM, lanes), lambda b: (b, 0, 0)),
        out_shape=jax.ShapeDtypeStruct((dbsz, B_HEAD_DIM, lanes), _F32),
        compiler_params=_cparams(("arbitrary",), 40),
    )(qbd, ckv, nkv, lane_consts)


def _overlap(n_cmp, n_sel, rows, cols):
    start = np.arange(n_cmp) * CMP_STRIDE
    end = start + CMP_LEN - 1
    s0 = np.arange(n_sel) * SEL_BLOCK
    s1 = s0 + SEL_BLOCK - 1
    m = np.zeros((rows, cols), np.float32)
    m[:n_cmp, :n_sel] = (start[:, None] <= s1[None, :]) & (end[:, None] >= s0[None, :])
    return m


def _expand_mat_t(n_blk, t_len, chunk):
    key_blk = np.arange(t_len) // SEL_BLOCK
    m = (key_blk[:, None] == np.arange(n_blk)[None, :]).astype(np.float32)
    return jnp.asarray(m.reshape(t_len // chunk, chunk, n_blk), dtype=_MXU_DTYPE)


def kernel(x_prompt, x_sample, cache_nsa_kv, cache_nsa_win, cache_shared_win, page_table,
           norm_attn, norm_mlp, a_w_in, a_q_gain, a_k_gain, a_cmp_pe, a_cmp_w1, a_cmp_w2, a_w_out,
           kv_norm, kv_w, kv_k_gain, b_w_q, b_q_gain, b_sinks, b_w_out, mlp_w1, mlp_w2):
    bsz, t_len, dm = x_prompt.shape
    dbsz, t_new, _ = x_sample.shape
    n_a = a_w_in.shape[0]
    depth = norm_attn.shape[0]
    n_pool, page_len = cache_nsa_kv.shape[1:3]
    past = page_table.shape[1] * page_len
    d, g, rep = A_HEAD_DIM, A_KV_GROUPS, A_REP
    hq = A_HEADS * d
    hkv = 6 * g * d
    mp = bsz * t_len
    ms = dbsz * t_new
    m_total = mp + ms
    wdt = _MXU_DTYPE
    slopes_a = jnp.asarray(_alibi_slopes(A_HEADS))

    h = jnp.concatenate([x_prompt.reshape(mp, dm), x_sample.reshape(ms, dm)], axis=0)
    rows_p, rows_s, win_p, win_s = [], [], [], []
    kv_sh = None
    for layer in range(depth):
        if layer < n_a:
            a = layer
            w_in = a_w_in[a]
            qt_all = _rms_mm(h, norm_attn[layer], w_in[:, :hq].astype(wdt), epi="headnorm", transpose_out=True,
                             gain=jnp.tile(a_q_gain[a], A_HEADS), flag=jnp.ones((hq,), _F32), out_dtype=wdt)
            one, zero = jnp.ones((g * d,), _F32), jnp.zeros((g * d,), _F32)
            kgain = a_k_gain[a]
            kv_gain = jnp.concatenate([one, one, jnp.tile(kgain[1], g), one, jnp.tile(kgain[2], g), one])
            kv_flag = jnp.concatenate([zero, zero, one, zero, one, zero])
            kv_all = _rms_mm(h, norm_attn[layer], w_in[:, hq:hq + hkv].astype(wdt), epi="headnorm",
                             gain=kv_gain, flag=kv_flag)
            n_gate = 3 * A_HEADS
            w_gate = jnp.pad(w_in[:, hq + hkv:], ((0, 0), (0, LANES - n_gate))).astype(wdt)
            gates = _rms_mm(h, norm_attn[layer], w_gate, epi="sigmoid")[:, :n_gate]

            w1cat = (a_cmp_w1[a].reshape(2, CMP_RATIO, CMP_STRIDE, d, d).transpose(0, 2, 3, 1, 4)
                     .reshape(2, CMP_STRIDE * d, CMP_RATIO * d).astype(wdt))
            pe2 = jnp.pad(a_cmp_pe[a].reshape(2, CMP_RATIO, CMP_STRIDE * d),
                          ((0, 0), (0, SUBLANES - CMP_RATIO), (0, 0))).astype(wdt)
            w2 = a_cmp_w2[a].astype(wdt)
            kgain0 = kgain[0].reshape(1, d)

            kc, vct = _cmp_prompt(kv_all, w1cat, pe2, w2, kgain0, bsz=bsz, t_len=t_len)
            n_chunk = t_len // CMP_STRIDE
            n_sel_p = -(-t_len // SEL_BLOCK)
            n_blk_p = -(-n_sel_p // SUBLANES) * SUBLANES
            ovt_p = jnp.asarray(_overlap(n_chunk - CMP_RATIO + 1, n_sel_p, n_chunk, n_blk_p).T)
            chunk = 512
            et = _expand_mat_t(n_blk_p, t_len, chunk)
            gates_t = (gates[:mp].reshape(bsz, t_len, 3, g, rep).transpose(0, 3, 2, 4, 1)
                       .reshape(bsz, g, 3 * rep, t_len))
            o_all = _nsa_prompt(qt_all, kv_all, gates_t, kc, vct, ovt_p, et, slopes_a,
                                bsz=bsz, t_len=t_len, m_total=m_total, chunk=chunk)

            q_s = (qt_all[:, mp:].astype(_F32).reshape(g, rep, d, dbsz, t_new).transpose(3, 0, 4, 1, 2)
                   .reshape(dbsz, g, t_new * rep, d))
            gt_s = (gates[mp:].reshape(dbsz, t_new, 3, g, rep).transpose(0, 3, 1, 4, 2)
                    .reshape(dbsz, g, t_new * rep, 3))
            kv_s = kv_all[mp:].reshape(dbsz, t_new, 6, g, d)
            new_rows = lambda slot: jnp.pad(kv_s[:, :, slot].transpose(0, 2, 1, 3),
                                            ((0, 0), (0, 0), (0, SUBLANES - t_new), (0, 0)))
            cache_rows = cache_nsa_kv[a].reshape(n_pool, page_len * 4 * g, d)
            cwin = cache_nsa_win[a]
            wbuf = cwin.shape[1]
            n_chunk_s = past // CMP_STRIDE
            n_sel_s = -(-(past + t_new) // SEL_BLOCK)
            n_blk_l = -(-n_sel_s // LANES) * LANES
            ov_s = jnp.asarray(_overlap(n_chunk_s - CMP_RATIO + 1, n_sel_s, n_chunk_s, n_blk_l))
            o_s = _nsa_sample(page_table, slopes_a, cache_rows, q_s, gt_s, new_rows(2), new_rows(3), new_rows(4),
                              new_rows(5), cwin.reshape(dbsz, wbuf * 2 * g, d), w1cat, pe2, w2, kgain0, ov_s,
                              past=past, t_new=t_new, page_len=page_len)
            o_s = o_s.reshape(dbsz, g, t_new, rep, d).transpose(0, 2, 1, 3, 4).reshape(ms, hq).astype(wdt)
            o_all = lax.dynamic_update_slice(o_all, o_s, (mp, 0))
            w_out = a_w_out[a]

            rows_p.append(kv_all[:mp, :4 * g * d].reshape(bsz, t_len, 4, g, d))
            rows_s.append(kv_all[mp:, :4 * g * d].reshape(dbsz, t_new, 4, g, d))
            wp = kv_all[:mp, 4 * g * d:].reshape(bsz, t_len, 2, g, d)
            win_p.append(wp[:, -min(A_WINDOW, t_len):])
            ws = kv_all[mp:, 4 * g * d:].reshape(dbsz, t_new, 2, g, d)
            win_s.append(jnp.concatenate([cwin, ws], axis=1)[:, -wbuf:])
        else:
            bl = layer - n_a
            hb = B_HEADS * B_HEAD_DIM
            nk = B_KV_HEADS * B_HEAD_DIM
            qt_all = _rms_mm(h, norm_attn[layer], b_w_q[bl].astype(wdt), epi="headnorm", hd=B_HEAD_DIM,
                             transpose_out=True, gain=jnp.tile(b_q_gain[bl], B_HEADS), flag=jnp.ones((hb,), _F32),
                             out_dtype=wdt)
            o_all = _swa_prompt(qt_all, kv_sh, b_sinks[bl], bsz=bsz, t_len=t_len, m_total=m_total)

            q5 = (qt_all[:, mp:].astype(_F32).reshape(B_KV_HEADS, B_REP, B_HEAD_DIM, dbsz, t_new)
                  .transpose(3, 0, 2, 1, 4).reshape(dbsz, B_KV_HEADS, B_HEAD_DIM, B_REP * t_new))
            qbd = (q5[:, :, :, None, :] * jnp.eye(B_KV_HEADS, dtype=_F32)[None, :, None, :, None]
                   ).reshape(dbsz, nk, B_HEADS * t_new).astype(wdt)
            slopes_b = jnp.asarray(_alibi_slopes(B_HEADS))
            lane_consts = jnp.stack([jnp.repeat(slopes_b * LOG2E, t_new), jnp.repeat(b_sinks[bl] * LOG2E, t_new),
                                     jnp.tile(jnp.arange(t_new, dtype=_F32), B_HEADS)]
                                    + [jnp.zeros((B_HEADS * t_new,), _F32)] * (SUBLANES - 3))
            nkv = jnp.pad(kv_sh[mp:].reshape(dbsz, t_new, 2 * nk), ((0, 0), (0, SUBLANES - t_new), (0, 0)))
            ckv = cache_shared_win.reshape(dbsz, cache_shared_win.shape[1], 2 * nk)
            o_s = _swa_sample(qbd, ckv, nkv, lane_consts, past=past)
            o_s = (o_s.reshape(dbsz, B_HEAD_DIM, B_HEADS, t_new).transpose(0, 3, 2, 1).reshape(ms, hb).astype(wdt))
            o_all = lax.dynamic_update_slice(o_all, o_s, (mp, 0))
            w_out = b_w_out[bl]

        h = _mm_res(o_all, w_out.astype(wdt), h)
        h = _mlp(h, norm_mlp[layer], mlp_w1[layer].astype(wdt), mlp_w2[layer].astype(wdt))
        if layer == n_a - 1:
            nk = B_KV_HEADS * B_HEAD_DIM
            kv_sh = _rms_mm(h, kv_norm, kv_w.astype(wdt), epi="headnorm", hd=B_HEAD_DIM,
                            gain=jnp.concatenate([jnp.tile(kv_k_gain, B_KV_HEADS), jnp.ones((nk,), _F32)]),
                            flag=jnp.concatenate([jnp.ones((nk,), _F32), jnp.zeros((nk,), _F32)]))

    wb = cache_shared_win.shape[1]
    kv_p = kv_sh[:mp].reshape(bsz, t_len, 2, B_KV_HEADS, B_HEAD_DIM)
    kv_s_new = kv_sh[mp:].reshape(dbsz, t_new, 2, B_KV_HEADS, B_HEAD_DIM)
    return (h[:mp].reshape(bsz, t_len, dm), h[mp:].reshape(dbsz, t_new, dm),
            jnp.stack(rows_p), jnp.stack(rows_s), jnp.stack(win_p), jnp.stack(win_s),
            kv_p[:, -min(B_WINDOW, t_len):], jnp.concatenate([cache_shared_win, kv_s_new], axis=1)[:, -wb:])
```

```python
import functools

import jax
import jax.numpy as jnp
import numpy as np
from jax import lax
from jax.experimental import pallas as pl
from jax.experimental.pallas import tpu as pltpu

A_HEADS = 16
A_HEAD_DIM = 128
A_KV_GROUPS = 2
A_REP = A_HEADS // A_KV_GROUPS
CMP_LEN = 32
CMP_STRIDE = 16
CMP_RATIO = CMP_LEN // CMP_STRIDE
SEL_BLOCK = 64
SEL_TOPK = 16
N_LOCAL_FORCED = 2
A_WINDOW = 512
B_HEADS = 32
B_HEAD_DIM = 64
B_KV_HEADS = 4
B_REP = B_HEADS // B_KV_HEADS
B_WINDOW = 128
NORM_EPS = 1e-6
NEG_INF = -1e30
FORCE_BONUS = 1e4

LANES = 128
SUBLANES = 8

LOG2E = float(np.log2(np.e))
MASKED_DIST = 1e33

_MXU_DTYPE = jnp.bfloat16
_F32 = jnp.float32


def _cparams(sem, vmem_mb):
    return pltpu.CompilerParams(dimension_semantics=sem, vmem_limit_bytes=vmem_mb * 1024 * 1024)


def _dot(a, b):
    return jnp.dot(a, b, preferred_element_type=_F32)


def _dot_nt(a, b):
    return lax.dot_general(a, b, (((1,), (1,)), ((), ())), preferred_element_type=_F32)


def _split3(a):
    hi = a.astype(_MXU_DTYPE)
    r1 = a - hi.astype(_F32)
    mid = r1.astype(_MXU_DTYPE)
    lo = (r1 - mid.astype(_F32)).astype(_MXU_DTYPE)
    return hi, mid, lo


def _dot_exact(a, b01):
    bm = b01.astype(_MXU_DTYPE)
    hi, mid, lo = _split3(a)
    return _dot(hi, bm) + _dot(mid, bm) + _dot(lo, bm)


def _dot_exact_rhs(a01, b):
    am = a01.astype(_MXU_DTYPE)
    hi, mid, lo = _split3(b)
    return _dot(am, hi) + _dot(am, mid) + _dot(am, lo)


def _idiv(x, n):
    if n & (n - 1) == 0:
        return lax.shift_right_logical(x, jnp.int32(n.bit_length() - 1))
    return x // n


def _pick_tile(m, want):
    if m <= want:
        return m
    for t in range(want - want % SUBLANES, 0, -SUBLANES):
        if m % t == 0:
            return t
    raise ValueError(f"no tile for {m}")


def _alibi_slopes(n_heads):
    return np.asarray(2.0 ** (-8.0 * np.arange(1, n_heads + 1) / n_heads), dtype=np.float32)


def _rms_rows(x, gain):
    ms = jnp.mean(x * x, axis=-1, keepdims=True)
    return x * lax.rsqrt(ms + NORM_EPS) * gain


def _head_norm(y, gain, flag, hd):
    outs = []
    for c in range(y.shape[1] // LANES):
        yc = y[:, c * LANES:(c + 1) * LANES]
        y2 = yc * yc
        if hd == LANES:
            ms = jnp.mean(y2, axis=-1, keepdims=True)
        else:
            row = _idiv(lax.broadcasted_iota(jnp.int32, (LANES, LANES), 0), hd)
            col = _idiv(lax.broadcasted_iota(jnp.int32, (LANES, LANES), 1), hd)
            ms = _dot_exact(y2, (row == col).astype(_F32)) * (1.0 / hd)
        yn = yc * lax.rsqrt(ms + NORM_EPS) * gain[:, c * LANES:(c + 1) * LANES]
        outs.append(jnp.where(flag[:, c * LANES:(c + 1) * LANES] > 0, yn, yc))
    return outs[0] if len(outs) == 1 else jnp.concatenate(outs, axis=1)


def _rms_mm_body(x_ref, gam_ref, w_ref, gain_ref, flag_ref, *refs, epi, hd, transpose_out):
    o_refs, xn_ref = refs[:-1], refs[-1]

    @pl.when(pl.program_id(1) == 0)
    def _():
        xn_ref[...] = _rms_rows(x_ref[...], gam_ref[...]).astype(xn_ref.dtype)

    y = _dot(xn_ref[...], w_ref[...])
    if epi == "headnorm":
        y = _head_norm(y, gain_ref[...], flag_ref[...], hd)
    elif epi == "sigmoid":
        y = jax.nn.sigmoid(y)
    if transpose_out:
        y = y.T
    col = 0
    for o_ref in o_refs:
        o_ref[...] = y[:, col:col + o_ref.shape[1]].astype(o_ref.dtype)
        col += o_ref.shape[1]


def _rms_mm(x, gamma, w, *, epi="none", gain=None, flag=None, hd=LANES, out_dtype=_F32, transpose_out=False,
            split=None, tm=512, tn=2048):
    m, k = x.shape
    n = w.shape[1]
    tm = _pick_tile(m, tm)
    tn = _pick_tile(n, tn)
    if gain is None:
        gain = jnp.ones((n,), _F32)
        flag = jnp.zeros((n,), _F32)
    if transpose_out:
        out_spec = pl.BlockSpec((tn, tm), lambda i, j: (j, i))
        out_shape = jax.ShapeDtypeStruct((n, m), out_dtype)
    elif split is not None:
        assert tn == n and sum(split) == n
        out_spec = [pl.BlockSpec((tm, ns), lambda i, j: (i, 0)) for ns in split]
        out_shape = [jax.ShapeDtypeStruct((m, ns), out_dtype) for ns in split]
    else:
        out_spec = pl.BlockSpec((tm, tn), lambda i, j: (i, j))
        out_shape = jax.ShapeDtypeStruct((m, n), out_dtype)
    return pl.pallas_call(
        functools.partial(_rms_mm_body, epi=epi, hd=hd, transpose_out=transpose_out),
        grid=(m // tm, n // tn),
        in_specs=[
            pl.BlockSpec((tm, k), lambda i, j: (i, 0)),
            pl.BlockSpec((1, k), lambda i, j: (0, 0)),
            pl.BlockSpec((k, tn), lambda i, j: (0, j)),
            pl.BlockSpec((1, tn), lambda i, j: (0, j)),
            pl.BlockSpec((1, tn), lambda i, j: (0, j)),
        ],
        out_specs=out_spec,
        out_shape=out_shape,
        scratch_shapes=[pltpu.VMEM((tm, k), _MXU_DTYPE)],
        compiler_params=_cparams(("arbitrary", "arbitrary"), 56),
    )(x, gamma.reshape(1, k), w, gain.reshape(1, n), flag.reshape(1, n))


def _mm_res_body(a_ref, w_ref, h_ref, o_ref):
    o_ref[...] = h_ref[...] + _dot(a_ref[...], w_ref[...])


def _mm_res(a, w, h, *, tm=512, tn=2048):
    m, k = a.shape
    n = w.shape[1]
    tm = _pick_tile(m, tm)
    tn = _pick_tile(n, tn)
    return pl.pallas_call(
        _mm_res_body,
        grid=(m // tm, n // tn),
        in_specs=[
            pl.BlockSpec((tm, k), lambda i, j: (i, 0)),
            pl.BlockSpec((k, tn), lambda i, j: (0, j)),
            pl.BlockSpec((tm, tn), lambda i, j: (i, j)),
        ],
        out_specs=pl.BlockSpec((tm, tn), lambda i, j: (i, j)),
        out_shape=jax.ShapeDtypeStruct((m, n), _F32),
        compiler_params=_cparams(("arbitrary", "arbitrary"), 56),
    )(a, w, h)


def _mlp_body(h_ref, gam_ref, w1_ref, w2_ref, o_ref, xn_ref):
    @pl.when(pl.program_id(1) == 0)
    def _():
        h = h_ref[...]
        xn_ref[...] = _rms_rows(h, gam_ref[...]).astype(xn_ref.dtype)
        o_ref[...] = h

    a = jnp.square(jnp.maximum(_dot(xn_ref[...], w1_ref[...]), 0.0))
    o_ref[...] += _dot(a.astype(_MXU_DTYPE), w2_ref[...])


def _mlp(h, gamma, w1, w2, *, tm=1024, tf=512):
    m, d = h.shape
    f = w1.shape[1]
    tm = _pick_tile(m, tm)
    tf = _pick_tile(f, tf)
    return pl.pallas_call(
        _mlp_body,
        grid=(m // tm, f // tf),
        in_specs=[
            pl.BlockSpec((tm, d), lambda i, j: (i, 0)),
            pl.BlockSpec((1, d), lambda i, j: (0, 0)),
            pl.BlockSpec((d, tf), lambda i, j: (0, j)),
            pl.BlockSpec((tf, d), lambda i, j: (j, 0)),
        ],
        out_specs=pl.BlockSpec((tm, d), lambda i, j: (i, 0)),
        out_shape=jax.ShapeDtypeStruct((m, d), _F32),
        scratch_shapes=[pltpu.VMEM((tm, d), _MXU_DTYPE)],
        compiler_params=_cparams(("arbitrary", "arbitrary"), 56),
    )(h, gamma.reshape(1, d), w1, w2)


def _block_scores(imp, blk, cur, n_sel):
    valid = blk <= cur
    forced = (blk == 0) | (((cur - blk) < N_LOCAL_FORCED) & valid)
    score = jnp.where(valid, imp + FORCE_BONUS * forced.astype(_F32), NEG_INF)
    return jnp.where(blk < n_sel, score, -jnp.inf)


def _select_blocks_rows(imp, qpos, n_sel, between=()):
    blk = lax.broadcasted_iota(jnp.int32, imp.shape, 1)
    s = _block_scores(imp, blk, _idiv(qpos, SEL_BLOCK), n_sel)
    sel = jnp.zeros(imp.shape, _F32)
    pending = list(between)
    for _ in range(min(SEL_TOPK, n_sel)):
        hit = blk == jnp.argmax(s, axis=-1, keepdims=True).astype(jnp.int32)
        sel = jnp.where(hit, 1.0, sel)
        s = jnp.where(hit, -jnp.inf, s)
        if pending:
            pending.pop(0)()
    for work in pending:
        work()
    return sel


def _select_blocks_cols(imp_t, qpos, n_sel):
    blk = lax.broadcasted_iota(jnp.int32, imp_t.shape, 0)
    s = _block_scores(imp_t, blk, _idiv(qpos, SEL_BLOCK), n_sel)
    beaten = jnp.zeros(imp_t.shape, _F32)
    for j in range(n_sel):
        sj = s[j:j + 1, :]
        tie = jnp.where(blk > j, 1.0, 0.0)
        beaten = beaten + jnp.where(sj > s, 1.0, 0.0) + jnp.where(sj == s, tie, 0.0)
    return jnp.where((beaten < min(SEL_TOPK, n_sel)) & (blk < n_sel), 1.0, 0.0)


def _compress_x2(x2, w1cat, pe2, w2, n_chunk):
    return _compress_finish(_dot(x2.astype(_MXU_DTYPE), w1cat), w1cat, pe2, w2, n_chunk)


def _compress_finish(h, w1cat, pe2, w2, n_chunk):
    d = A_HEAD_DIM
    c = _dot(pe2, w1cat)
    part0 = h[:, :d] + c[0:1, :d]
    part1 = h[:, d:] + c[1:2, d:]
    pieces = []
    for s in range(h.shape[0] // n_chunk):
        p1 = part1[s * n_chunk:(s + 1) * n_chunk]
        pieces.append(part0[s * n_chunk:(s + 1) * n_chunk] + pltpu.roll(p1, n_chunk - 1, 0))
    hh = pieces[0] if len(pieces) == 1 else jnp.concatenate(pieces, axis=0)
    return _dot(jax.nn.gelu(hh).astype(_MXU_DTYPE), w2)


def _softmax_cols(s_t, distm, slope2, scale2, sink2=None):
    sc = s_t * scale2 - slope2 * distm
    m = jnp.max(sc, axis=0, keepdims=True)
    if sink2 is not None:
        m = jnp.maximum(m, sink2)
    e = jnp.exp2(sc - m)
    l = jnp.sum(e, axis=0, keepdims=True)
    if sink2 is not None:
        l = l + jnp.exp2(sink2 - m)
    return e, m, l


def _cmp_prompt_body(kraw_ref, vraw_ref, w1_ref, pe_ref, w2_ref, kg_ref, kc_ref, vct_ref, *, n_chunk):
    for slot, src in enumerate((kraw_ref, vraw_ref)):
        x2 = jnp.concatenate([src[pl.ds(s, n_chunk, stride=CMP_STRIDE), :] for s in range(CMP_STRIDE)], axis=1)
        y = _compress_x2(x2, w1_ref[slot], pe_ref[slot], w2_ref[slot], n_chunk)
        if slot == 0:
            kc_ref[...] = _rms_rows(y, kg_ref[...])
        else:
            vct_ref[...] = y.T


def _cmp_prompt(kv, w1cat, pe2, w2, kgain0, *, bsz, t_len):
    n_chunk = t_len // CMP_STRIDE
    d = A_HEAD_DIM
    g = A_KV_GROUPS
    return pl.pallas_call(
        functools.partial(_cmp_prompt_body, n_chunk=n_chunk),
        grid=(bsz, g),
        in_specs=[
            pl.BlockSpec((t_len, d), lambda b, gi: (b, gi)),
            pl.BlockSpec((t_len, d), lambda b, gi: (b, g + gi)),
            pl.BlockSpec(w1cat.shape, lambda b, gi: (0, 0, 0)),
            pl.BlockSpec(pe2.shape, lambda b, gi: (0, 0, 0)),
            pl.BlockSpec(w2.shape, lambda b, gi: (0, 0, 0)),
            pl.BlockSpec((1, d), lambda b, gi: (0, 0)),
        ],
        out_specs=[pl.BlockSpec((None, None, n_chunk, d), lambda b, gi: (b, gi, 0, 0)),
                   pl.BlockSpec((None, None, d, n_chunk), lambda b, gi: (b, gi, 0, 0))],
        out_shape=[jax.ShapeDtypeStruct((bsz, g, n_chunk, d), _F32), jax.ShapeDtypeStruct((bsz, g, d, n_chunk), _F32)],
        compiler_params=_cparams(("arbitrary", "arbitrary"), 32),
    )(kv, kv, w1cat, pe2, w2, kgain0)


def _nsa_prompt_body(slope_ref, qt_ref, gt_ref, kc_ref, vct_ref, ks_ref, vs_ref, kw_ref, vw_ref, ovt_ref, et_ref,
                     o_ref, ksb, vst, kwb, vwt, *, tq, t_len, n_cmp, n_sel, chunk):
    gi = pl.program_id(1)
    i = pl.program_id(2)
    d = A_HEAD_DIM
    rep = A_REP
    scale2 = d ** -0.5 * LOG2E
    n_chunks = t_len // chunk
    n_wblk = A_WINDOW // tq + 1

    @pl.when(i == 0)
    def _():
        ksb[...] = ks_ref[...].astype(ksb.dtype)
        kwb[...] = kw_ref[...].astype(kwb.dtype)
        for c in range(n_chunks):
            vst[c] = vs_ref[c * chunk:(c + 1) * chunk, :].T.astype(vst.dtype)
        for c in range(t_len // tq):
            vwt[c] = vw_ref[c * tq:(c + 1) * tq, :].T.astype(vwt.dtype)

    t0 = i * tq
    qt = qt_ref[...]
    qst = jnp.concatenate([qt[r * d:(r + 1) * d, :] for r in range(rep)], axis=1)
    tpos = t0 + lax.broadcasted_iota(jnp.int32, (1, tq), 1)
    slope2 = [slope_ref[gi * rep + r] * LOG2E for r in range(rep)]
    head = lambda a, r: a[:, r * tq:(r + 1) * tq]

    def masked_dist(valid, dist):
        return jnp.where(valid, dist.astype(_F32), MASKED_DIST)

    n_c = kc_ref.shape[0]
    cidx = lax.broadcasted_iota(jnp.int32, (n_c, 1), 0)
    cend = cidx * CMP_STRIDE + (CMP_LEN - 1)
    distm = masked_dist((cend <= tpos) & (cidx < n_cmp), tpos - cend)
    s_t = _dot(kc_ref[...].astype(_MXU_DTYPE), qst)
    any_valid = (tpos >= CMP_LEN - 1).astype(_F32)
    p_heads = []
    for r in range(rep):
        e, _, l = _softmax_cols(head(s_t, r), distm, slope2[r], scale2)
        p_heads.append(e * (any_valid / l))
    p_sum = p_heads[0]
    for r in range(1, rep):
        p_sum = p_sum + p_heads[r]
    o_cmp = _dot(vct_ref[...].astype(_MXU_DTYPE), jnp.concatenate(p_heads, axis=1).astype(_MXU_DTYPE))
    imp_t = _dot_exact_rhs(ovt_ref[...], p_sum)
    sel_t = _select_blocks_cols(imp_t, tpos, n_sel).astype(_MXU_DTYPE)

    n_live = (t0 + tq + chunk - 1) // chunk

    def sel_step(c, carry):
        m_i, l_i, acc = carry
        base = pl.multiple_of(c * chunk, chunk)
        kpos = base + lax.broadcasted_iota(jnp.int32, (chunk, 1), 0)
        picked = _dot(et_ref[c], sel_t)
        distm = masked_dist((picked > 0.5) & (kpos <= tpos), tpos - kpos)
        s_t = _dot(ksb[pl.ds(base, chunk), :], qst)
        es, ms, ls = [], [], []
        for r in range(rep):
            sc = head(s_t, r) * scale2 - slope2[r] * distm
            m_new = jnp.maximum(head(m_i, r), jnp.max(sc, axis=0, keepdims=True))
            e = jnp.exp2(sc - m_new)
            es.append(e.astype(_MXU_DTYPE))
            ms.append(m_new)
            ls.append(jnp.sum(e, axis=0, keepdims=True))
        m_new = jnp.concatenate(ms, axis=1)
        alpha = jnp.exp2(m_i - m_new)
        l_new = alpha * l_i + jnp.concatenate(ls, axis=1)
        acc = alpha * acc + _dot(vst[c], jnp.concatenate(es, axis=1))
        return m_new, l_new, acc

    init = (jnp.full((1, rep * tq), NEG_INF, _F32), jnp.zeros((1, rep * tq), _F32), jnp.zeros((d, rep * tq), _F32))
    _, l_s, acc_s = lax.fori_loop(0, n_live, sel_step, init)
    o_sel = acc_s * (1.0 / l_s)

    wlen = n_wblk * tq
    wb0 = jnp.maximum(i - (n_wblk - 1), 0)
    start = pl.multiple_of(wb0 * tq, tq)
    kpos = start + lax.broadcasted_iota(jnp.int32, (wlen, 1), 0)
    dw = tpos - kpos
    distm = masked_dist((dw >= 0) & (dw < A_WINDOW), dw)
    s_t = _dot(kwb[pl.ds(start, wlen), :], qst)
    es, ls = [], []
    for r in range(rep):
        e, _, l = _softmax_cols(head(s_t, r), distm, slope2[r], scale2)
        es.append(e.astype(_MXU_DTYPE))
        ls.append(l)
    vw = jnp.concatenate([vwt[wb0 + j] for j in range(n_wblk)], axis=1)
    o_win = _dot(vw, jnp.concatenate(es, axis=1)) * (1.0 / jnp.concatenate(ls, axis=1))

    gt = gt_ref[...]
    for r in range(rep):
        o_r = (gt[r:r + 1] * head(o_cmp, r) + gt[rep + r:rep + r + 1] * head(o_sel, r)
               + gt[2 * rep + r:2 * rep + r + 1] * head(o_win, r))
        o_ref[:, r * d:(r + 1) * d] = o_r.T.astype(o_ref.dtype)


def _nsa_prompt(qt_all, rows, win, gates_t, kc, vct, ovt, et, slopes, *, bsz, t_len, tq=256, chunk=512):
    d = A_HEAD_DIM
    g = A_KV_GROUPS
    rep = A_REP
    nq = t_len // tq
    n_chunk = kc.shape[2]
    n_cmp = n_chunk - CMP_RATIO + 1
    n_sel = -(-t_len // SEL_BLOCK)
    assert t_len % tq == 0 and A_WINDOW % tq == 0 and t_len >= A_WINDOW + tq
    assert t_len % chunk == 0 and chunk % SEL_BLOCK == 0
    kvspec = lambda col: pl.BlockSpec((t_len, d), lambda b, gi, i, col=col: (b, col + gi))
    return pl.pallas_call(
        functools.partial(_nsa_prompt_body, tq=tq, t_len=t_len, n_cmp=n_cmp, n_sel=n_sel, chunk=chunk),
        grid=(bsz, g, nq),
        in_specs=[
            pl.BlockSpec(memory_space=pltpu.SMEM),
            pl.BlockSpec((rep * d, tq), lambda b, gi, i: (gi, b * nq + i)),
            pl.BlockSpec((None, None, 3 * rep, tq), lambda b, gi, i: (b, gi, 0, i)),
            pl.BlockSpec((None, None, n_chunk, d), lambda b, gi, i: (b, gi, 0, 0)),
            pl.BlockSpec((None, None, d, n_chunk), lambda b, gi, i: (b, gi, 0, 0)),
            kvspec(2 * g), kvspec(3 * g), kvspec(0), kvspec(g),
            pl.BlockSpec(ovt.shape, lambda b, gi, i: (0, 0)),
            pl.BlockSpec(et.shape, lambda b, gi, i: (0, 0, 0)),
        ],
        out_specs=pl.BlockSpec((tq, rep * d), lambda b, gi, i: (b * nq + i, gi)),
        out_shape=jax.ShapeDtypeStruct((bsz * t_len, g * rep * d), _MXU_DTYPE),
        scratch_shapes=[pltpu.VMEM((t_len, d), _MXU_DTYPE), pltpu.VMEM((t_len // chunk, d, chunk), _MXU_DTYPE),
                        pltpu.VMEM((t_len, d), _MXU_DTYPE), pltpu.VMEM((t_len // tq, d, tq), _MXU_DTYPE)],
        compiler_params=_cparams(("arbitrary", "arbitrary", "arbitrary"), 48),
    )(slopes, qt_all, gates_t, kc, vct, rows, rows, win, win, ovt, et)


def _nsa_sample_body(pt_ref, slope_ref, *refs, pps, past, t_new, n_cmp, n_sel):
    pages = refs[:pps]
    (q_ref, gt_ref, ksn_ref, vsn_ref, kwn_ref, vwn_ref, cwin_ref, w1_ref, pe_ref, w2_ref, kg_ref, ov_ref,
     o_ref, x2k, x2v, ksb, vsb, emat) = refs[pps:]
    b = pl.program_id(0)
    pp = pl.program_id(1)
    n_pp = pl.num_programs(1)
    d = A_HEAD_DIM
    g = A_KV_GROUPS
    rep = A_REP
    rows = t_new * rep
    scale = d ** -0.5
    sg_stride = 4 * g
    page_len = pages[0].shape[0] // sg_stride
    cpp = page_len // CMP_STRIDE
    assert cpp == SUBLANES
    n_chunk = past // CMP_STRIDE
    n_rb = n_chunk // cpp

    @pl.when((b == 0) & (pp == 0))
    def _():
        blk = lax.broadcasted_iota(jnp.int32, emat.shape, 0)
        key = _idiv(lax.broadcasted_iota(jnp.int32, emat.shape, 1), SEL_BLOCK)
        emat[...] = (blk == key).astype(emat.dtype)

    for pi in range(pps):
        pg = pages[pi]
        page_no = pp * pps + pi
        for gi in range(g):
            rb = gi * n_rb + page_no
            for slot, dst in ((0, x2k), (1, x2v)):
                a = pg[pl.ds(slot * g + gi, page_len, stride=sg_stride), :]
                dst[rb] = pltpu.einshape("csd->scd", a.reshape(cpp, CMP_STRIDE, d)).reshape(page_len, d)
            krow = pl.multiple_of(page_no * page_len, page_len)
            ksb[gi, pl.ds(krow, page_len), :] = pg[pl.ds(2 * g + gi, page_len, stride=sg_stride), :].astype(ksb.dtype)
            vsb[gi, pl.ds(krow, page_len), :] = pg[pl.ds(3 * g + gi, page_len, stride=sg_stride), :].astype(vsb.dtype)

    @pl.when(pp == n_pp - 1)
    def _():
        def chunk_rows(src, rb0, rb1):
            slabs = []
            for rb in range(rb0, rb1):
                slab = src[rb]
                slabs.append(jnp.concatenate([slab[s * cpp:(s + 1) * cpp] for s in range(CMP_STRIDE)], axis=1))
            return jnp.concatenate(slabs, axis=0).astype(_MXU_DTYPE)

        kc_all = _rms_rows(_compress_finish(_dot(chunk_rows(x2k, 0, g * n_rb), w1_ref[0]), w1_ref[0], pe_ref[0],
                                            w2_ref[0], n_chunk), kg_ref[...])
        rowi = lax.broadcasted_iota(jnp.int32, (rows, 1), 0)
        qpos = past + _idiv(rowi, rep)
        head_r = rowi - _idiv(rowi, rep) * rep
        pad_new = jnp.zeros((LANES - ksn_ref.shape[1], d), _F32)

        def padded(ref, gi):
            return jnp.concatenate([ref[gi], pad_new], axis=0).astype(_MXU_DTYPE)

        def biased(s, valid, dist, slope):
            return jnp.where(valid, s * scale - slope * dist, NEG_INF)

        def two_part(s_p, s_n, v_p, v_n):
            m = jnp.maximum(jnp.max(s_p, axis=-1, keepdims=True), jnp.max(s_n, axis=-1, keepdims=True))
            e_p = jnp.exp(s_p - m)
            e_n = jnp.exp(s_n - m)
            l = jnp.sum(e_p, axis=-1, keepdims=True) + jnp.sum(e_n, axis=-1, keepdims=True)
            return (_dot(e_p.astype(_MXU_DTYPE), v_p) + _dot(e_n.astype(_MXU_DTYPE), v_n)) / l

        qs, slopes, p_cs, imps = [], [], [], []
        cidx = lax.broadcasted_iota(jnp.int32, (1, n_chunk), 1)
        cend = cidx * CMP_STRIDE + (CMP_LEN - 1)
        for gi in range(g):
            q = q_ref[gi].astype(_MXU_DTYPE)
            slope = jnp.zeros((rows, 1), _F32)
            for r in range(rep):
                slope = jnp.where(head_r == r, slope_ref[gi * rep + r], slope)
            kc = kc_all[gi * n_chunk:(gi + 1) * n_chunk].astype(_MXU_DTYPE)
            s = biased(_dot_nt(q, kc), (cend <= qpos) & (cidx < n_cmp), (qpos - cend).astype(_F32), slope)
            e = jnp.exp(s - jnp.max(s, axis=-1, keepdims=True))
            p_c = e / jnp.sum(e, axis=-1, keepdims=True) * (qpos >= CMP_LEN - 1).astype(_F32)
            p3 = p_c.reshape(t_new, rep, n_chunk)
            p_sum = jnp.broadcast_to(jnp.sum(p3, axis=1, keepdims=True), p3.shape).reshape(rows, n_chunk)
            imps.append(_dot_exact(p_sum, ov_ref[...]))
            p_cs.append(p_c.astype(_MXU_DTYPE))
            qs.append(q)
            slopes.append(slope)

        npos = past + lax.broadcasted_iota(jnp.int32, (1, LANES), 1)
        wbuf = cwin_ref.shape[0] // (2 * g)
        wpos = past - wbuf + lax.broadcasted_iota(jnp.int32, (1, wbuf), 1)
        o_wins, hv = [None] * g, []

        def window_branch(gi):
            q, slope = qs[gi], slopes[gi]
            kw = cwin_ref[pl.ds(gi, wbuf, stride=2 * g), :].astype(_MXU_DTYPE)
            vw = cwin_ref[pl.ds(g + gi, wbuf, stride=2 * g), :].astype(_MXU_DTYPE)
            dw = qpos - wpos
            s_p = biased(_dot_nt(q, kw), (dw >= 0) & (dw < A_WINDOW), dw.astype(_F32), slope)
            dn = qpos - npos
            s_n = biased(_dot_nt(q, padded(kwn_ref, gi)), (dn >= 0) & (dn < A_WINDOW), dn.astype(_F32), slope)
            o_wins[gi] = two_part(s_p, s_n, vw, padded(vwn_ref, gi))

        n_piece = 8
        per_piece = g * n_rb // n_piece
        fillers = [functools.partial(window_branch, gi) for gi in range(g)]
        fillers += [lambda j=j: hv.append(_dot(chunk_rows(x2v, j * per_piece, (j + 1) * per_piece), w1_ref[1]))
                    for j in range(n_piece)]
        selm_all = _select_blocks_rows(jnp.concatenate(imps, axis=0), jnp.concatenate([qpos] * g, axis=0), n_sel,
                                       between=fillers)
        vc_all = _compress_finish(jnp.concatenate(hv, axis=0), w1_ref[1], pe_ref[1], w2_ref[1], n_chunk)

        kpos = lax.broadcasted_iota(jnp.int32, (1, past), 1)
        new_blk = past // SEL_BLOCK
        for gi in range(g):
            q, slope = qs[gi], slopes[gi]
            selm = selm_all[gi * rows:(gi + 1) * rows]
            o_cmp = _dot(p_cs[gi], vc_all[gi * n_chunk:(gi + 1) * n_chunk].astype(_MXU_DTYPE))

            picked = _dot(selm.astype(_MXU_DTYPE), emat[...])
            s_p = biased(_dot_nt(q, ksb[gi]), (picked > 0.5) & (kpos <= qpos), (qpos - kpos).astype(_F32), slope)
            sel_new = selm[:, new_blk:new_blk + 1] > 0.5
            s_n = biased(_dot_nt(q, padded(ksn_ref, gi)), sel_new & (npos <= qpos), (qpos - npos).astype(_F32), slope)
            o_sel = two_part(s_p, s_n, vsb[gi], padded(vsn_ref, gi))

            gt = gt_ref[gi]
            o_ref[gi] = gt[:, 0:1] * o_cmp + gt[:, 1:2] * o_sel + gt[:, 2:3] * o_wins[gi]


def _nsa_sample(page_table, slopes, cache_rows, q_s, gt_s, ksn, vsn, kwn, vwn, cwin, w1cat, pe2, w2, kgain0, ov_s,
                *, past, t_new, page_len, pps=8):
    dbsz, n_pages = page_table.shape
    d = A_HEAD_DIM
    g = A_KV_GROUPS
    rows = t_new * A_REP
    n_chunk = past // CMP_STRIDE
    n_cmp = n_chunk - CMP_RATIO + 1
    n_sel = -(-(past + t_new) // SEL_BLOCK)
    n_blk_l = ov_s.shape[1]
    assert past % CMP_STRIDE == 0 and t_new < CMP_STRIDE and n_pages % pps == 0 and past % SEL_BLOCK == 0
    assert t_new <= SEL_BLOCK and n_sel <= n_blk_l
    page_rows = page_len * 4 * g
    cpp = page_len // CMP_STRIDE

    def page_spec(pi):
        return pl.BlockSpec((None, page_rows, d), lambda b, pp, pt, pi=pi: (pt[b * n_pages + pp * pps + pi], 0, 0))

    per_b = lambda shape: pl.BlockSpec((None,) + shape, lambda b, pp, pt: (b,) + (0,) * len(shape))
    whole = lambda a: pl.BlockSpec(a.shape, lambda b, pp, pt: (0,) * a.ndim)
    grid_spec = pltpu.PrefetchScalarGridSpec(
        num_scalar_prefetch=1,
        grid=(dbsz, n_pages // pps),
        in_specs=[pl.BlockSpec(memory_space=pltpu.SMEM)] + [page_spec(pi) for pi in range(pps)] + [
            per_b((g, rows, d)), per_b((g, rows, 3)),
            per_b(ksn.shape[1:]), per_b(vsn.shape[1:]), per_b(kwn.shape[1:]), per_b(vwn.shape[1:]),
            per_b(cwin.shape[1:]),
            whole(w1cat), whole(pe2), whole(w2), whole(kgain0), whole(ov_s),
        ],
        out_specs=per_b((g, rows, d)),
        scratch_shapes=[
            pltpu.VMEM((g * n_pages, CMP_STRIDE * cpp, d), _F32), pltpu.VMEM((g * n_pages, CMP_STRIDE * cpp, d), _F32),
            pltpu.VMEM((g, past, d), _MXU_DTYPE), pltpu.VMEM((g, past, d), _MXU_DTYPE),
            pltpu.VMEM((n_blk_l, past), _MXU_DTYPE),
        ],
    )
    return pl.pallas_call(
        functools.partial(_nsa_sample_body, pps=pps, past=past, t_new=t_new, n_cmp=n_cmp, n_sel=n_sel),
        grid_spec=grid_spec,
        out_shape=jax.ShapeDtypeStruct((dbsz, g, rows, d), _F32),
        compiler_params=_cparams(("arbitrary", "arbitrary"), 60),
    )(page_table.reshape(-1), slopes, *([cache_rows] * pps), q_s, gt_s, ksn, vsn, kwn, vwn, cwin,
      w1cat, pe2, w2, kgain0, ov_s)


def _swa_prompt_body(sink_ref, qt_ref, kv_ref, o_ref, kb, vt, *, tq, t_len):
    i = pl.program_id(1)
    hd = B_HEAD_DIM
    nk = B_KV_HEADS * hd
    scale2 = hd ** -0.5 * LOG2E
    slopes = _alibi_slopes(B_HEADS)
    n_wblk = (B_WINDOW + tq) // LANES
    wlen = n_wblk * LANES

    @pl.when(i == 0)
    def _():
        kb[...] = kv_ref[:, :nk].astype(kb.dtype)
        for c in range(t_len // LANES):
            vt[c] = kv_ref[c * LANES:(c + 1) * LANES, nk:].T.astype(vt.dtype)

    t0 = i * tq
    wb0 = jnp.maximum(i * (tq // LANES) - B_WINDOW // LANES, 0)
    start = pl.multiple_of(wb0 * LANES, LANES)
    tpos = t0 + lax.broadcasted_iota(jnp.int32, (1, tq), 1)
    kpos = start + lax.broadcasted_iota(jnp.int32, (wlen, 1), 0)
    dw = tpos - kpos
    distm = jnp.where((dw >= 0) & (dw < B_WINDOW), dw.astype(_F32), MASKED_DIST)
    kwin = kb[pl.ds(start, wlen), :]
    vwin_t = jnp.concatenate([vt[wb0 + j] for j in range(n_wblk)], axis=1)
    qt = qt_ref[...]
    zeros = jnp.zeros((hd, tq), qt.dtype)
    for kh in range(B_KV_HEADS):
        pb = (kh // 2) * LANES
        blocks = []
        for r in range(B_REP):
            h = kh * B_REP + r
            qh = qt[h * hd:(h + 1) * hd, :]
            blocks.append(jnp.concatenate([qh, zeros] if kh % 2 == 0 else [zeros, qh], axis=0))
        s_t = _dot(kwin[:, pb:pb + LANES], jnp.concatenate(blocks, axis=1))
        ps = []
        for r in range(B_REP):
            h = kh * B_REP + r
            e, _, l = _softmax_cols(s_t[:, r * tq:(r + 1) * tq], distm, float(slopes[h]) * LOG2E, scale2,
                                    sink2=sink_ref[h] * LOG2E)
            ps.append((e * (1.0 / l)).astype(_MXU_DTYPE))
        o_t = _dot(vwin_t[kh * hd:(kh + 1) * hd, :], jnp.concatenate(ps, axis=1))
        for p in range(B_REP // 2):
            pair = jnp.concatenate([o_t[:, (2 * p) * tq:(2 * p + 1) * tq], o_t[:, (2 * p + 1) * tq:(2 * p + 2) * tq]], axis=0)
            col0 = (kh * B_REP + 2 * p) * hd
            o_ref[:, col0:col0 + LANES] = pair.T.astype(o_ref.dtype)


def _swa_prompt(qt_all, kv_all, sinks, *, bsz, t_len, tq=256):
    nq = t_len // tq
    hq = B_HEADS * B_HEAD_DIM
    nk = B_KV_HEADS * B_HEAD_DIM
    assert t_len % tq == 0 and t_len >= B_WINDOW + tq and tq % LANES == 0 and B_WINDOW % LANES == 0
    return pl.pallas_call(
        functools.partial(_swa_prompt_body, tq=tq, t_len=t_len),
        grid=(bsz, nq),
        in_specs=[
            pl.BlockSpec(memory_space=pltpu.SMEM),
            pl.BlockSpec((hq, tq), lambda b, i: (0, b * nq + i)),
            pl.BlockSpec((t_len, kv_all.shape[1]), lambda b, i: (b, 0)),
        ],
        out_specs=pl.BlockSpec((tq, hq), lambda b, i: (b * nq + i, 0)),
        out_shape=jax.ShapeDtypeStruct((bsz * t_len, hq), _MXU_DTYPE),
        scratch_shapes=[pltpu.VMEM((t_len, nk), _MXU_DTYPE), pltpu.VMEM((t_len // LANES, nk, LANES), _MXU_DTYPE)],
        compiler_params=_cparams(("arbitrary", "arbitrary"), 40),
    )(sinks, qt_all, kv_all)


def _swa_sample_body(qbd_ref, ckv_ref, nkv_ref, lane_ref, o_ref, *, bt, past, wbuf, tk):
    hd = B_HEAD_DIM
    nk = B_KV_HEADS * hd
    scale2 = hd ** -0.5 * LOG2E
    lanes = qbd_ref.shape[2]
    slope2 = lane_ref[0:1, :]
    sink2 = lane_ref[1:2, :]
    qpos = past + lane_ref[2:3, :].astype(jnp.int32)
    kpos = past - wbuf + lax.broadcasted_iota(jnp.int32, (tk, 1), 0)
    dw = qpos - kpos
    distm = jnp.where((dw >= 0) & (dw < B_WINDOW), dw.astype(_F32), MASKED_DIST)
    lane_kh = _idiv(lax.broadcasted_iota(jnp.int32, (hd, lanes), 1), lanes // B_KV_HEADS)
    pad = jnp.zeros((tk - wbuf - nkv_ref.shape[1], 2 * nk), _F32)
    for bi in range(bt):
        kv = jnp.concatenate([ckv_ref[bi], nkv_ref[bi], pad], axis=0)
        s_t = _dot(kv[:, :nk].astype(_MXU_DTYPE), qbd_ref[bi])
        e, _, l = _softmax_cols(s_t, distm, slope2, scale2, sink2=sink2)
        p_t = (e * (1.0 / l)).astype(_MXU_DTYPE)
        o_full = _dot(kv[:, nk:].T.astype(_MXU_DTYPE), p_t)
        o = jnp.zeros((hd, lanes), _F32)
        for kh in range(B_KV_HEADS):
            o = jnp.where(lane_kh == kh, o_full[kh * hd:(kh + 1) * hd, :], o)
        o_ref[bi] = o


def _swa_sample(qbd, ckv, nkv, lane_consts, *, past, bt=16):
    dbsz, nk, lanes = qbd.shape
    wbuf = ckv.shape[1]
    tk = -(-(wbuf + nkv.shape[1]) // LANES) * LANES
    bt = _pick_tile(dbsz, bt)
    return pl.pallas_call(
        functools.partial(_swa_sample_body, bt=bt, past=past, wbuf=wbuf, tk=tk),
        grid=(dbsz // bt,),
        in_specs=[
            pl.BlockSpec((bt,) + qbd.shape[1:], lambda b: (b, 0, 0)),
            pl.BlockSpec((bt,) + ckv.shape[1:], lambda b: (b, 0, 0)),
            pl.BlockSpec((bt,) + nkv.shape[1:], lambda b: (b, 0, 0)),
            pl.BlockSpec(lane_consts.shape, lambda b: (0, 0)),
        ],
        out_specs=pl.BlockSpec((bt, B_HEAD_DIM, lanes), lambda b: (b, 0, 0)),
        out_shape=jax.ShapeDtypeStruct((dbsz, B_HEAD_DIM, lanes), _F32),
        compiler_params=_cparams(("arbitrary",), 40),
    )(qbd, ckv, nkv, lane_consts)


def _overlap(n_cmp, n_sel, rows, cols):
    start = np.arange(n_cmp) * CMP_STRIDE
    end = start + CMP_LEN - 1
    s0 = np.arange(n_sel) * SEL_BLOCK
    s1 = s0 + SEL_BLOCK - 1
    m = np.zeros((rows, cols), np.float32)
    m[:n_cmp, :n_sel] = (start[:, None] <= s1[None, :]) & (end[:, None] >= s0[None, :])
    return m


def _expand_mat_t(n_blk, t_len, chunk):
    key_blk = np.arange(t_len) // SEL_BLOCK
    m = (key_blk[:, None] == np.arange(n_blk)[None, :]).astype(np.float32)
    return jnp.asarray(m.reshape(t_len // chunk, chunk, n_blk), dtype=_MXU_DTYPE)


def kernel(x_prompt, x_sample, cache_nsa_kv, cache_nsa_win, cache_shared_win, page_table,
           norm_attn, norm_mlp, a_w_in, a_q_gain, a_k_gain, a_cmp_pe, a_cmp_w1, a_cmp_w2, a_w_out,
           kv_norm, kv_w, kv_k_gain, b_w_q, b_q_gain, b_sinks, b_w_out, mlp_w1, mlp_w2):
    bsz, t_len, dm = x_prompt.shape
    dbsz, t_new, _ = x_sample.shape
    n_a = a_w_in.shape[0]
    depth = norm_attn.shape[0]
    n_pool, page_len = cache_nsa_kv.shape[1:3]
    past = page_table.shape[1] * page_len
    d, g, rep = A_HEAD_DIM, A_KV_GROUPS, A_REP
    hq = A_HEADS * d
    hkv = 6 * g * d
    mp = bsz * t_len
    ms = dbsz * t_new
    wdt = _MXU_DTYPE
    slopes_a = jnp.asarray(_alibi_slopes(A_HEADS))

    hs2 = [x_prompt.reshape(mp, dm), x_sample.reshape(ms, dm)]
    both = lambda fn: [fn(hh) for hh in hs2]
    rows_p, rows_s, win_p, win_s = [], [], [], []
    kv_sh = None
    for layer in range(depth):
        if layer < n_a:
            a = layer
            w_in = a_w_in[a]
            wq = w_in[:, :hq].astype(wdt)
            qt_p, qt_s = both(lambda hh: _rms_mm(hh, norm_attn[layer], wq, epi="headnorm", transpose_out=True,
                                                 gain=jnp.tile(a_q_gain[a], A_HEADS), flag=jnp.ones((hq,), _F32),
                                                 out_dtype=wdt))
            one, zero = jnp.ones((g * d,), _F32), jnp.zeros((g * d,), _F32)
            kgain = a_k_gain[a]
            kv_gain = jnp.concatenate([one, one, jnp.tile(kgain[1], g), one, jnp.tile(kgain[2], g), one])
            kv_flag = jnp.concatenate([zero, zero, one, zero, one, zero])
            wkv = w_in[:, hq:hq + hkv].astype(wdt)
            (rw_p, wn_p), (rw_s, wn_s) = both(lambda hh: _rms_mm(hh, norm_attn[layer], wkv, epi="headnorm", gain=kv_gain,
                                                                 flag=kv_flag, split=(4 * g * d, 2 * g * d)))
            n_gate = 3 * A_HEADS
            w_gate = jnp.pad(w_in[:, hq + hkv:], ((0, 0), (0, LANES - n_gate))).astype(wdt)
            gates_p, gates_s = both(lambda hh: _rms_mm(hh, norm_attn[layer], w_gate, epi="sigmoid")[:, :n_gate])

            w1cat = (a_cmp_w1[a].reshape(2, CMP_RATIO, CMP_STRIDE, d, d).transpose(0, 2, 3, 1, 4)
                     .reshape(2, CMP_STRIDE * d, CMP_RATIO * d).astype(wdt))
            pe2 = jnp.pad(a_cmp_pe[a].reshape(2, CMP_RATIO, CMP_STRIDE * d),
                          ((0, 0), (0, SUBLANES - CMP_RATIO), (0, 0))).astype(wdt)
            w2 = a_cmp_w2[a].astype(wdt)
            kgain0 = kgain[0].reshape(1, d)

            kc, vct = _cmp_prompt(rw_p, w1cat, pe2, w2, kgain0, bsz=bsz, t_len=t_len)
            n_chunk = t_len // CMP_STRIDE
            n_sel_p = -(-t_len // SEL_BLOCK)
            n_blk_p = -(-n_sel_p // SUBLANES) * SUBLANES
            ovt_p = jnp.asarray(_overlap(n_chunk - CMP_RATIO + 1, n_sel_p, n_chunk, n_blk_p).T)
            chunk = 512
            et = _expand_mat_t(n_blk_p, t_len, chunk)
            gates_t = (gates_p.reshape(bsz, t_len, 3, g, rep).transpose(0, 3, 2, 4, 1)
                       .reshape(bsz, g, 3 * rep, t_len))
            o_p = _nsa_prompt(qt_p, rw_p, wn_p, gates_t, kc, vct, ovt_p, et, slopes_a,
                              bsz=bsz, t_len=t_len, chunk=chunk)

            q_s = (qt_s.astype(_F32).reshape(g, rep, d, dbsz, t_new).transpose(3, 0, 4, 1, 2)
                   .reshape(dbsz, g, t_new * rep, d))
            gt_s = (gates_s.reshape(dbsz, t_new, 3, g, rep).transpose(0, 3, 1, 4, 2)
                    .reshape(dbsz, g, t_new * rep, 3))
            kv_s = jnp.concatenate([rw_s, wn_s], axis=1).reshape(dbsz, t_new, 6, g, d)
            new_rows = lambda slot: jnp.pad(kv_s[:, :, slot].transpose(0, 2, 1, 3),
                                            ((0, 0), (0, 0), (0, SUBLANES - t_new), (0, 0)))
            cache_rows = cache_nsa_kv[a].reshape(n_pool, page_len * 4 * g, d)
            cwin = cache_nsa_win[a]
            wbuf = cwin.shape[1]
            n_chunk_s = past // CMP_STRIDE
            n_sel_s = -(-(past + t_new) // SEL_BLOCK)
            n_blk_l = -(-n_sel_s // LANES) * LANES
            ov_s = jnp.asarray(_overlap(n_chunk_s - CMP_RATIO + 1, n_sel_s, n_chunk_s, n_blk_l))
            o_s = _nsa_sample(page_table, slopes_a, cache_rows, q_s, gt_s, new_rows(2), new_rows(3), new_rows(4),
                              new_rows(5), cwin.reshape(dbsz, wbuf * 2 * g, d), w1cat, pe2, w2, kgain0, ov_s,
                              past=past, t_new=t_new, page_len=page_len)
            o_s = o_s.reshape(dbsz, g, t_new, rep, d).transpose(0, 2, 1, 3, 4).reshape(ms, hq).astype(wdt)
            w_out = a_w_out[a]

            rows_p.append(rw_p.reshape(bsz, t_len, 4, g, d))
            rows_s.append(rw_s.reshape(dbsz, t_new, 4, g, d))
            win_p.append(wn_p.reshape(bsz, t_len, 2, g, d)[:, -min(A_WINDOW, t_len):])
            win_s.append(jnp.concatenate([cwin, wn_s.reshape(dbsz, t_new, 2, g, d)], axis=1)[:, -wbuf:])
        else:
            bl = layer - n_a
            hb = B_HEADS * B_HEAD_DIM
            nk = B_KV_HEADS * B_HEAD_DIM
            wq = b_w_q[bl].astype(wdt)
            qt_p, qt_s = both(lambda hh: _rms_mm(hh, norm_attn[layer], wq, epi="headnorm", hd=B_HEAD_DIM,
                                                 transpose_out=True, gain=jnp.tile(b_q_gain[bl], B_HEADS),
                                                 flag=jnp.ones((hb,), _F32), out_dtype=wdt))
            o_p = _swa_prompt(qt_p, kv_sh[0], b_sinks[bl], bsz=bsz, t_len=t_len)

            q5 = (qt_s.astype(_F32).reshape(B_KV_HEADS, B_REP, B_HEAD_DIM, dbsz, t_new)
                  .transpose(3, 0, 2, 1, 4).reshape(dbsz, B_KV_HEADS, B_HEAD_DIM, B_REP * t_new))
            qbd = (q5[:, :, :, None, :] * jnp.eye(B_KV_HEADS, dtype=_F32)[None, :, None, :, None]
                   ).reshape(dbsz, nk, B_HEADS * t_new).astype(wdt)
            slopes_b = jnp.asarray(_alibi_slopes(B_HEADS))
            lane_consts = jnp.stack([jnp.repeat(slopes_b * LOG2E, t_new), jnp.repeat(b_sinks[bl] * LOG2E, t_new),
                                     jnp.tile(jnp.arange(t_new, dtype=_F32), B_HEADS)]
                                    + [jnp.zeros((B_HEADS * t_new,), _F32)] * (SUBLANES - 3))
            nkv = jnp.pad(kv_sh[1].reshape(dbsz, t_new, 2 * nk), ((0, 0), (0, SUBLANES - t_new), (0, 0)))
            ckv = cache_shared_win.reshape(dbsz, cache_shared_win.shape[1], 2 * nk)
            o_s = _swa_sample(qbd, ckv, nkv, lane_consts, past=past)
            o_s = (o_s.reshape(dbsz, B_HEAD_DIM, B_HEADS, t_new).transpose(0, 3, 2, 1).reshape(ms, hb).astype(wdt))
            w_out = b_w_out[bl]

        w_o, w_1, w_2 = w_out.astype(wdt), mlp_w1[layer].astype(wdt), mlp_w2[layer].astype(wdt)
        hs2 = [_mlp(_mm_res(o, w_o, hh), norm_mlp[layer], w_1, w_2) for o, hh in zip((o_p, o_s), hs2)]
        if layer == n_a - 1:
            nk = B_KV_HEADS * B_HEAD_DIM
            wkv_sh = kv_w.astype(wdt)
            kv_sh = both(lambda hh: _rms_mm(
                hh, kv_norm, wkv_sh, epi="headnorm", hd=B_HEAD_DIM,
                gain=jnp.concatenate([jnp.tile(kv_k_gain, B_KV_HEADS), jnp.ones((nk,), _F32)]),
                flag=jnp.concatenate([jnp.ones((nk,), _F32), jnp.zeros((nk,), _F32)])))

    wb = cache_shared_win.shape[1]
    kv_p = kv_sh[0].reshape(bsz, t_len, 2, B_KV_HEADS, B_HEAD_DIM)
    kv_s_new = kv_sh[1].reshape(dbsz, t_new, 2, B_KV_HEADS, B_HEAD_DIM)
    return (hs2[0].reshape(bsz, t_len, dm), hs2[1].reshape(dbsz, t_new, dm),
            jnp.stack(rows_p), jnp.stack(rows_s), jnp.stack(win_p), jnp.stack(win_s),
            kv_p[:, -min(B_WINDOW, t_len):], jnp.concatenate([cache_shared_win, kv_s_new], axis=1)[:, -wb:])
```

```python
import functools

import jax
import jax.numpy as jnp
import numpy as np
from jax import lax
from jax.experimental import pallas as pl
from jax.experimental.pallas import tpu as pltpu

A_HEADS = 16
A_HEAD_DIM = 128
A_KV_GROUPS = 2
A_REP = A_HEADS // A_KV_GROUPS
CMP_LEN = 32
CMP_STRIDE = 16
CMP_RATIO = CMP_LEN // CMP_STRIDE
SEL_BLOCK = 64
SEL_TOPK = 16
N_LOCAL_FORCED = 2
A_WINDOW = 512
B_HEADS = 32
B_HEAD_DIM = 64
B_KV_HEADS = 4
B_REP = B_HEADS // B_KV_HEADS
B_WINDOW = 128
NORM_EPS = 1e-6
NEG_INF = -1e30
FORCE_BONUS = 1e4

LANES = 128
SUBLANES = 8

LOG2E = float(np.log2(np.e))
MASKED_DIST = 1e33

_MXU_DTYPE = jnp.bfloat16
_F32 = jnp.float32


def _cparams(sem, vmem_mb):
    return pltpu.CompilerParams(dimension_semantics=sem, vmem_limit_bytes=vmem_mb * 1024 * 1024)


def _dot(a, b):
    return jnp.dot(a, b, preferred_element_type=_F32)


def _dot_nt(a, b):
    return lax.dot_general(a, b, (((1,), (1,)), ((), ())), preferred_element_type=_F32)


def _split3(a):
    hi = a.astype(_MXU_DTYPE)
    r1 = a - hi.astype(_F32)
    mid = r1.astype(_MXU_DTYPE)
    lo = (r1 - mid.astype(_F32)).astype(_MXU_DTYPE)
    return hi, mid, lo


def _dot_exact(a, b01):
    bm = b01.astype(_MXU_DTYPE)
    hi, mid, lo = _split3(a)
    return _dot(hi, bm) + _dot(mid, bm) + _dot(lo, bm)


def _dot_exact_rhs(a01, b):
    am = a01.astype(_MXU_DTYPE)
    hi, mid, lo = _split3(b)
    return _dot(am, hi) + _dot(am, mid) + _dot(am, lo)


def _idiv(x, n):
    if n & (n - 1) == 0:
        return lax.shift_right_logical(x, jnp.int32(n.bit_length() - 1))
    return x // n


def _pick_tile(m, want):
    if m <= want:
        return m
    for t in range(want - want % SUBLANES, 0, -SUBLANES):
        if m % t == 0:
            return t
    raise ValueError(f"no tile for {m}")


def _alibi_slopes(n_heads):
    return np.asarray(2.0 ** (-8.0 * np.arange(1, n_heads + 1) / n_heads), dtype=np.float32)


def _rms_rows(x, gain):
    ms = jnp.mean(x * x, axis=-1, keepdims=True)
    return x * lax.rsqrt(ms + NORM_EPS) * gain


def _head_norm(y, gain, flag, hd):
    outs = []
    for c in range(y.shape[1] // LANES):
        yc = y[:, c * LANES:(c + 1) * LANES]
        y2 = yc * yc
        if hd == LANES:
            ms = jnp.mean(y2, axis=-1, keepdims=True)
        else:
            row = _idiv(lax.broadcasted_iota(jnp.int32, (LANES, LANES), 0), hd)
            col = _idiv(lax.broadcasted_iota(jnp.int32, (LANES, LANES), 1), hd)
            ms = _dot_exact(y2, (row == col).astype(_F32)) * (1.0 / hd)
        yn = yc * lax.rsqrt(ms + NORM_EPS) * gain[:, c * LANES:(c + 1) * LANES]
        outs.append(jnp.where(flag[:, c * LANES:(c + 1) * LANES] > 0, yn, yc))
    return outs[0] if len(outs) == 1 else jnp.concatenate(outs, axis=1)


def _rms_mm_body(x_ref, gam_ref, w_ref, gain_ref, flag_ref, *refs, epi, hd, transpose_out, interleave):
    o_refs, xn_ref = refs[:-1], refs[-1]
    if interleave:
        o_refs, il_ref = o_refs[:-1], o_refs[-1]

    @pl.when(pl.program_id(1) == 0)
    def _():
        xn_ref[...] = _rms_rows(x_ref[...], gam_ref[...]).astype(xn_ref.dtype)

    y = _dot(xn_ref[...], w_ref[...])
    if epi == "headnorm":
        y = _head_norm(y, gain_ref[...], flag_ref[...], hd)
    elif epi == "sigmoid":
        y = jax.nn.sigmoid(y)
    if transpose_out:
        y = y.T
    col = 0
    for o_ref in o_refs:
        o_ref[...] = y[:, col:col + o_ref.shape[1]].astype(o_ref.dtype)
        col += o_ref.shape[1]
    if interleave:
        for c in range(interleave):
            il_ref[pl.ds(c, y.shape[0], stride=interleave), :] = y[:, c * LANES:(c + 1) * LANES]


def _rms_mm(x, gamma, w, *, epi="none", gain=None, flag=None, hd=LANES, out_dtype=_F32, transpose_out=False,
            split=None, interleave=0, tm=512, tn=2048):
    m, k = x.shape
    n = w.shape[1]
    tm = _pick_tile(m, tm)
    tn = _pick_tile(n, tn)
    if gain is None:
        gain = jnp.ones((n,), _F32)
        flag = jnp.zeros((n,), _F32)
    if transpose_out:
        out_spec = pl.BlockSpec((tn, tm), lambda i, j: (j, i))
        out_shape = jax.ShapeDtypeStruct((n, m), out_dtype)
    elif split is not None:
        assert tn == n and sum(split) == n
        out_spec = [pl.BlockSpec((tm, ns), lambda i, j: (i, 0)) for ns in split]
        out_shape = [jax.ShapeDtypeStruct((m, ns), out_dtype) for ns in split]
        if interleave:
            out_spec.append(pl.BlockSpec((tm * interleave, LANES), lambda i, j: (i, 0)))
            out_shape.append(jax.ShapeDtypeStruct((m * interleave, LANES), out_dtype))
    else:
        out_spec = pl.BlockSpec((tm, tn), lambda i, j: (i, j))
        out_shape = jax.ShapeDtypeStruct((m, n), out_dtype)
    return pl.pallas_call(
        functools.partial(_rms_mm_body, epi=epi, hd=hd, transpose_out=transpose_out, interleave=interleave),
        grid=(m // tm, n // tn),
        in_specs=[
            pl.BlockSpec((tm, k), lambda i, j: (i, 0)),
            pl.BlockSpec((1, k), lambda i, j: (0, 0)),
            pl.BlockSpec((k, tn), lambda i, j: (0, j)),
            pl.BlockSpec((1, tn), lambda i, j: (0, j)),
            pl.BlockSpec((1, tn), lambda i, j: (0, j)),
        ],
        out_specs=out_spec,
        out_shape=out_shape,
        scratch_shapes=[pltpu.VMEM((tm, k), _MXU_DTYPE)],
        compiler_params=_cparams(("arbitrary", "arbitrary"), 56),
    )(x, gamma.reshape(1, k), w, gain.reshape(1, n), flag.reshape(1, n))


def _mm_res_body(a_ref, w_ref, h_ref, o_ref):
    o_ref[...] = h_ref[...] + _dot(a_ref[...], w_ref[...])


def _mm_res(a, w, h, *, tm=512, tn=2048):
    m, k = a.shape
    n = w.shape[1]
    tm = _pick_tile(m, tm)
    tn = _pick_tile(n, tn)
    return pl.pallas_call(
        _mm_res_body,
        grid=(m // tm, n // tn),
        in_specs=[
            pl.BlockSpec((tm, k), lambda i, j: (i, 0)),
            pl.BlockSpec((k, tn), lambda i, j: (0, j)),
            pl.BlockSpec((tm, tn), lambda i, j: (i, j)),
        ],
        out_specs=pl.BlockSpec((tm, tn), lambda i, j: (i, j)),
        out_shape=jax.ShapeDtypeStruct((m, n), _F32),
        compiler_params=_cparams(("arbitrary", "arbitrary"), 56),
    )(a, w, h)


def _mlp_body(h_ref, gam_ref, w1_ref, w2_ref, o_ref, xn_ref):
    @pl.when(pl.program_id(1) == 0)
    def _():
        h = h_ref[...]
        xn_ref[...] = _rms_rows(h, gam_ref[...]).astype(xn_ref.dtype)
        o_ref[...] = h

    a = jnp.square(jnp.maximum(_dot(xn_ref[...], w1_ref[...]), 0.0))
    o_ref[...] += _dot(a.astype(_MXU_DTYPE), w2_ref[...])


def _mlp_cast_body(h_ref, gam_ref, w1_ref, w2_ref, o_ref, w1b_ref, w2b_ref, xn_ref):
    @pl.when(pl.program_id(0) == 0)
    def _():
        h = h_ref[...]
        xn_ref[...] = _rms_rows(h, gam_ref[...]).astype(xn_ref.dtype)
        o_ref[...] = h

    w1b = w1_ref[...].astype(w1b_ref.dtype)
    w2b = w2_ref[...].astype(w2b_ref.dtype)
    w1b_ref[...] = w1b
    w2b_ref[...] = w2b
    a = jnp.square(jnp.maximum(_dot(xn_ref[...], w1b), 0.0))
    o_ref[...] += _dot(a.astype(_MXU_DTYPE), w2b)


def _mlp_cast(h, gamma, w1, w2, *, tf=512):
    m, d = h.shape
    f = w1.shape[1]
    tf = _pick_tile(f, tf)
    return pl.pallas_call(
        _mlp_cast_body,
        grid=(f // tf,),
        in_specs=[
            pl.BlockSpec((m, d), lambda j: (0, 0)),
            pl.BlockSpec((1, d), lambda j: (0, 0)),
            pl.BlockSpec((d, tf), lambda j: (0, j)),
            pl.BlockSpec((tf, d), lambda j: (j, 0)),
        ],
        out_specs=[pl.BlockSpec((m, d), lambda j: (0, 0)),
                   pl.BlockSpec((d, tf), lambda j: (0, j)),
                   pl.BlockSpec((tf, d), lambda j: (j, 0))],
        out_shape=[jax.ShapeDtypeStruct((m, d), _F32), jax.ShapeDtypeStruct(w1.shape, _MXU_DTYPE),
                   jax.ShapeDtypeStruct(w2.shape, _MXU_DTYPE)],
        scratch_shapes=[pltpu.VMEM((m, d), _MXU_DTYPE)],
        compiler_params=_cparams(("arbitrary",), 56),
    )(h, gamma.reshape(1, d), w1, w2)


def _mlp(h, gamma, w1, w2, *, tm=1024, tf=512):
    m, d = h.shape
    f = w1.shape[1]
    tm = _pick_tile(m, tm)
    tf = _pick_tile(f, tf)
    return pl.pallas_call(
        _mlp_body,
        grid=(m // tm, f // tf),
        in_specs=[
            pl.BlockSpec((tm, d), lambda i, j: (i, 0)),
            pl.BlockSpec((1, d), lambda i, j: (0, 0)),
            pl.BlockSpec((d, tf), lambda i, j: (0, j)),
            pl.BlockSpec((tf, d), lambda i, j: (j, 0)),
        ],
        out_specs=pl.BlockSpec((tm, d), lambda i, j: (i, 0)),
        out_shape=jax.ShapeDtypeStruct((m, d), _F32),
        scratch_shapes=[pltpu.VMEM((tm, d), _MXU_DTYPE)],
        compiler_params=_cparams(("arbitrary", "arbitrary"), 56),
    )(h, gamma.reshape(1, d), w1, w2)


def _block_scores(imp, blk, cur, n_sel):
    valid = blk <= cur
    forced = (blk == 0) | (((cur - blk) < N_LOCAL_FORCED) & valid)
    score = jnp.where(valid, imp + FORCE_BONUS * forced.astype(_F32), NEG_INF)
    return jnp.where(blk < n_sel, score, -jnp.inf)


def _select_blocks_rows(imp, qpos, n_sel, between=()):
    blk = lax.broadcasted_iota(jnp.int32, imp.shape, 1)
    s = _block_scores(imp, blk, _idiv(qpos, SEL_BLOCK), n_sel)
    sel = jnp.zeros(imp.shape, _F32)
    pending = list(between)
    for _ in range(min(SEL_TOPK, n_sel)):
        hit = blk == jnp.argmax(s, axis=-1, keepdims=True).astype(jnp.int32)
        sel = jnp.where(hit, 1.0, sel)
        s = jnp.where(hit, -jnp.inf, s)
        if pending:
            pending.pop(0)()
    for work in pending:
        work()
    return sel


def _select_blocks_cols(imp_t, qpos, n_sel):
    blk = lax.broadcasted_iota(jnp.int32, imp_t.shape, 0)
    s = _block_scores(imp_t, blk, _idiv(qpos, SEL_BLOCK), n_sel)
    beaten = jnp.zeros(imp_t.shape, _F32)
    for j in range(n_sel):
        sj = s[j:j + 1, :]
        tie = jnp.where(blk > j, 1.0, 0.0)
        beaten = beaten + jnp.where(sj > s, 1.0, 0.0) + jnp.where(sj == s, tie, 0.0)
    return jnp.where((beaten < min(SEL_TOPK, n_sel)) & (blk < n_sel), 1.0, 0.0)


def _compress_x2(x2, w1cat, pe2, w2, n_chunk):
    return _compress_finish(_dot(x2.astype(_MXU_DTYPE), w1cat), w1cat, pe2, w2, n_chunk)


def _compress_finish(h, w1cat, pe2, w2, n_chunk):
    d = A_HEAD_DIM
    c = _dot(pe2, w1cat)
    part0 = h[:, :d] + c[0:1, :d]
    part1 = h[:, d:] + c[1:2, d:]
    pieces = []
    for s in range(h.shape[0] // n_chunk):
        p1 = part1[s * n_chunk:(s + 1) * n_chunk]
        pieces.append(part0[s * n_chunk:(s + 1) * n_chunk] + pltpu.roll(p1, n_chunk - 1, 0))
    hh = pieces[0] if len(pieces) == 1 else jnp.concatenate(pieces, axis=0)
    return _dot(jax.nn.gelu(hh).astype(_MXU_DTYPE), w2)


def _softmax_cols(s_t, distm, slope2, scale2, sink2=None):
    sc = s_t * scale2 - slope2 * distm
    m = jnp.max(sc, axis=0, keepdims=True)
    if sink2 is not None:
        m = jnp.maximum(m, sink2)
    e = jnp.exp2(sc - m)
    l = jnp.sum(e, axis=0, keepdims=True)
    if sink2 is not None:
        l = l + jnp.exp2(sink2 - m)
    return e, m, l


def _cmp_prompt_body(kraw_ref, vraw_ref, w1_ref, pe_ref, w2_ref, kg_ref, kc_ref, vct_ref, *, n_chunk):
    for slot, src in enumerate((kraw_ref, vraw_ref)):
        x2 = jnp.concatenate([src[pl.ds(s, n_chunk, stride=CMP_STRIDE), :] for s in range(CMP_STRIDE)], axis=1)
        y = _compress_x2(x2, w1_ref[slot], pe_ref[slot], w2_ref[slot], n_chunk)
        if slot == 0:
            kc_ref[...] = _rms_rows(y, kg_ref[...])
        else:
            vct_ref[...] = y.T


def _cmp_prompt(kv, w1cat, pe2, w2, kgain0, *, bsz, t_len):
    n_chunk = t_len // CMP_STRIDE
    d = A_HEAD_DIM
    g = A_KV_GROUPS
    return pl.pallas_call(
        functools.partial(_cmp_prompt_body, n_chunk=n_chunk),
        grid=(bsz, g),
        in_specs=[
            pl.BlockSpec((t_len, d), lambda b, gi: (b, gi)),
            pl.BlockSpec((t_len, d), lambda b, gi: (b, g + gi)),
            pl.BlockSpec(w1cat.shape, lambda b, gi: (0, 0, 0)),
            pl.BlockSpec(pe2.shape, lambda b, gi: (0, 0, 0)),
            pl.BlockSpec(w2.shape, lambda b, gi: (0, 0, 0)),
            pl.BlockSpec((1, d), lambda b, gi: (0, 0)),
        ],
        out_specs=[pl.BlockSpec((None, None, n_chunk, d), lambda b, gi: (b, gi, 0, 0)),
                   pl.BlockSpec((None, None, d, n_chunk), lambda b, gi: (b, gi, 0, 0))],
        out_shape=[jax.ShapeDtypeStruct((bsz, g, n_chunk, d), _F32), jax.ShapeDtypeStruct((bsz, g, d, n_chunk), _F32)],
        compiler_params=_cparams(("arbitrary", "arbitrary"), 32),
    )(kv, kv, w1cat, pe2, w2, kgain0)


def _nsa_prompt_body(slope_ref, qt_ref, gt_ref, kc_ref, vct_ref, ks_ref, vs_ref, kw_ref, vw_ref, ovt_ref, et_ref,
                     o_ref, ksb, vst, kwb, vwt, *, tq, t_len, n_cmp, n_sel, chunk):
    gi = pl.program_id(1)
    i = pl.program_id(2)
    d = A_HEAD_DIM
    rep = A_REP
    scale2 = d ** -0.5 * LOG2E
    n_chunks = t_len // chunk
    n_wblk = A_WINDOW // tq + 1

    @pl.when(i == 0)
    def _():
        ksb[...] = ks_ref[...].astype(ksb.dtype)
        kwb[...] = kw_ref[...].astype(kwb.dtype)
        for c in range(n_chunks):
            vst[c] = vs_ref[c * chunk:(c + 1) * chunk, :].T.astype(vst.dtype)
        for c in range(t_len // tq):
            vwt[c] = vw_ref[c * tq:(c + 1) * tq, :].T.astype(vwt.dtype)

    t0 = i * tq
    qt = qt_ref[...]
    qst = jnp.concatenate([qt[r * d:(r + 1) * d, :] for r in range(rep)], axis=1)
    tpos = t0 + lax.broadcasted_iota(jnp.int32, (1, tq), 1)
    slope2 = [slope_ref[gi * rep + r] * LOG2E for r in range(rep)]
    head = lambda a, r: a[:, r * tq:(r + 1) * tq]

    def masked_dist(valid, dist):
        return jnp.where(valid, dist.astype(_F32), MASKED_DIST)

    n_c = kc_ref.shape[0]
    cidx = lax.broadcasted_iota(jnp.int32, (n_c, 1), 0)
    cend = cidx * CMP_STRIDE + (CMP_LEN - 1)
    distm = masked_dist((cend <= tpos) & (cidx < n_cmp), tpos - cend)
    s_t = _dot(kc_ref[...].astype(_MXU_DTYPE), qst)
    any_valid = (tpos >= CMP_LEN - 1).astype(_F32)
    p_heads = []
    for r in range(rep):
        e, _, l = _softmax_cols(head(s_t, r), distm, slope2[r], scale2)
        p_heads.append(e * (any_valid / l))
    p_sum = p_heads[0]
    for r in range(1, rep):
        p_sum = p_sum + p_heads[r]
    o_cmp = _dot(vct_ref[...].astype(_MXU_DTYPE), jnp.concatenate(p_heads, axis=1).astype(_MXU_DTYPE))
    imp_t = _dot_exact_rhs(ovt_ref[...], p_sum)
    sel_t = _select_blocks_cols(imp_t, tpos, n_sel).astype(_MXU_DTYPE)

    n_live = (t0 + tq + chunk - 1) // chunk

    def sel_step(c, carry):
        m_i, l_i, acc = carry
        base = pl.multiple_of(c * chunk, chunk)
        kpos = base + lax.broadcasted_iota(jnp.int32, (chunk, 1), 0)
        picked = _dot(et_ref[c], sel_t)
        distm = masked_dist((picked > 0.5) & (kpos <= tpos), tpos - kpos)
        s_t = _dot(ksb[pl.ds(base, chunk), :], qst)
        es, ms, ls = [], [], []
        for r in range(rep):
            sc = head(s_t, r) * scale2 - slope2[r] * distm
            m_new = jnp.maximum(head(m_i, r), jnp.max(sc, axis=0, keepdims=True))
            e = jnp.exp2(sc - m_new)
            es.append(e.astype(_MXU_DTYPE))
            ms.append(m_new)
            ls.append(jnp.sum(e, axis=0, keepdims=True))
        m_new = jnp.concatenate(ms, axis=1)
        alpha = jnp.exp2(m_i - m_new)
        l_new = alpha * l_i + jnp.concatenate(ls, axis=1)
        acc = alpha * acc + _dot(vst[c], jnp.concatenate(es, axis=1))
        return m_new, l_new, acc

    init = (jnp.full((1, rep * tq), NEG_INF, _F32), jnp.zeros((1, rep * tq), _F32), jnp.zeros((d, rep * tq), _F32))
    _, l_s, acc_s = lax.fori_loop(0, n_live, sel_step, init)
    o_sel = acc_s * (1.0 / l_s)

    wlen = n_wblk * tq
    wb0 = jnp.maximum(i - (n_wblk - 1), 0)
    start = pl.multiple_of(wb0 * tq, tq)
    kpos = start + lax.broadcasted_iota(jnp.int32, (wlen, 1), 0)
    dw = tpos - kpos
    distm = masked_dist((dw >= 0) & (dw < A_WINDOW), dw)
    s_t = _dot(kwb[pl.ds(start, wlen), :], qst)
    es, ls = [], []
    for r in range(rep):
        e, _, l = _softmax_cols(head(s_t, r), distm, slope2[r], scale2)
        es.append(e.astype(_MXU_DTYPE))
        ls.append(l)
    vw = jnp.concatenate([vwt[wb0 + j] for j in range(n_wblk)], axis=1)
    o_win = _dot(vw, jnp.concatenate(es, axis=1)) * (1.0 / jnp.concatenate(ls, axis=1))

    gt = gt_ref[...]
    for r in range(rep):
        o_r = (gt[r:r + 1] * head(o_cmp, r) + gt[rep + r:rep + r + 1] * head(o_sel, r)
               + gt[2 * rep + r:2 * rep + r + 1] * head(o_win, r))
        o_ref[:, r * d:(r + 1) * d] = o_r.T.astype(o_ref.dtype)


def _nsa_prompt(qt_all, rows, win, gates_t, kc, vct, ovt, et, slopes, *, bsz, t_len, tq=256, chunk=512):
    d = A_HEAD_DIM
    g = A_KV_GROUPS
    rep = A_REP
    nq = t_len // tq
    n_chunk = kc.shape[2]
    n_cmp = n_chunk - CMP_RATIO + 1
    n_sel = -(-t_len // SEL_BLOCK)
    assert t_len % tq == 0 and A_WINDOW % tq == 0 and t_len >= A_WINDOW + tq
    assert t_len % chunk == 0 and chunk % SEL_BLOCK == 0
    kvspec = lambda col: pl.BlockSpec((t_len, d), lambda b, gi, i, col=col: (b, col + gi))
    return pl.pallas_call(
        functools.partial(_nsa_prompt_body, tq=tq, t_len=t_len, n_cmp=n_cmp, n_sel=n_sel, chunk=chunk),
        grid=(bsz, g, nq),
        in_specs=[
            pl.BlockSpec(memory_space=pltpu.SMEM),
            pl.BlockSpec((rep * d, tq), lambda b, gi, i: (gi, b * nq + i)),
            pl.BlockSpec((None, None, 3 * rep, tq), lambda b, gi, i: (b, gi, 0, i)),
            pl.BlockSpec((None, None, n_chunk, d), lambda b, gi, i: (b, gi, 0, 0)),
            pl.BlockSpec((None, None, d, n_chunk), lambda b, gi, i: (b, gi, 0, 0)),
            kvspec(2 * g), kvspec(3 * g), kvspec(0), kvspec(g),
            pl.BlockSpec(ovt.shape, lambda b, gi, i: (0, 0)),
            pl.BlockSpec(et.shape, lambda b, gi, i: (0, 0, 0)),
        ],
        out_specs=pl.BlockSpec((tq, rep * d), lambda b, gi, i: (b * nq + i, gi)),
        out_shape=jax.ShapeDtypeStruct((bsz * t_len, g * rep * d), _MXU_DTYPE),
        scratch_shapes=[pltpu.VMEM((t_len, d), _MXU_DTYPE), pltpu.VMEM((t_len // chunk, d, chunk), _MXU_DTYPE),
                        pltpu.VMEM((t_len, d), _MXU_DTYPE), pltpu.VMEM((t_len // tq, d, tq), _MXU_DTYPE)],
        compiler_params=_cparams(("arbitrary", "arbitrary", "arbitrary"), 48),
    )(slopes, qt_all, gates_t, kc, vct, rows, rows, win, win, ovt, et)


def _nsa_sample_body(pt_ref, slope_ref, *refs, pps, past, t_new, n_cmp, n_sel):
    pages = refs[:pps]
    (q_ref, gt_ref, ksn_ref, vsn_ref, kwn_ref, vwn_ref, cwin_ref, w1_ref, pe_ref, w2_ref, kg_ref, ov_ref,
     o_ref, x2, hcmp, ksb, vsb, emat) = refs[pps:]
    b = pl.program_id(0)
    pp = pl.program_id(1)
    n_pp = pl.num_programs(1)
    d = A_HEAD_DIM
    g = A_KV_GROUPS
    rep = A_REP
    rows = t_new * rep
    scale = d ** -0.5
    sg_stride = 4 * g
    page_len = pages[0].shape[0] // sg_stride
    cpp = page_len // CMP_STRIDE
    assert cpp == SUBLANES
    n_chunk = past // CMP_STRIDE
    n_rb = n_chunk // cpp

    @pl.when((b == 0) & (pp == 0))
    def _():
        blk = lax.broadcasted_iota(jnp.int32, emat.shape, 0)
        key = _idiv(lax.broadcasted_iota(jnp.int32, emat.shape, 1), SEL_BLOCK)
        emat[...] = (blk == key).astype(emat.dtype)

    for pi in range(pps):
        pg = pages[pi]
        page_no = pp * pps + pi
        for gi in range(g):
            for slot in range(2):
                a = pg[pl.ds(slot * g + gi, page_len, stride=sg_stride), :]
                for c in range(cpp):
                    for hs in range(CMP_STRIDE // SUBLANES):
                        p0 = c * CMP_STRIDE + hs * SUBLANES
                        x2[slot, gi * pps + pi, pl.ds(hs * SUBLANES * cpp + c, SUBLANES, stride=cpp), :] = a[p0:p0 + SUBLANES]
            krow = pl.multiple_of(page_no * page_len, page_len)
            ksb[gi, pl.ds(krow, page_len), :] = pg[pl.ds(2 * g + gi, page_len, stride=sg_stride), :].astype(ksb.dtype)
            vsb[gi, pl.ds(krow, page_len), :] = pg[pl.ds(3 * g + gi, page_len, stride=sg_stride), :].astype(vsb.dtype)
    step_chunks = pps * cpp
    for slot in range(2):
        slabs = []
        for rb in range(g * pps):
            slab = x2[slot, rb]
            slabs.append(jnp.concatenate([slab[s * cpp:(s + 1) * cpp] for s in range(CMP_STRIDE)], axis=1))
        h_step = _dot(jnp.concatenate(slabs, axis=0).astype(_MXU_DTYPE), w1_ref[slot])
        for gi in range(g):
            hrow = pl.multiple_of(gi * n_chunk + pp * step_chunks, step_chunks)
            hcmp[slot, pl.ds(hrow, step_chunks), :] = h_step[gi * step_chunks:(gi + 1) * step_chunks]

    @pl.when(pp == n_pp - 1)
    def _():
        kc_all = _rms_rows(_compress_finish(hcmp[0], w1_ref[0], pe_ref[0], w2_ref[0], n_chunk), kg_ref[...])
        vc_all = _compress_finish(hcmp[1], w1_ref[1], pe_ref[1], w2_ref[1], n_chunk)
        rowi = lax.broadcasted_iota(jnp.int32, (rows, 1), 0)
        qpos = past + _idiv(rowi, rep)
        head_r = rowi - _idiv(rowi, rep) * rep
        pad_new = jnp.zeros((LANES - ksn_ref.shape[1], d), _F32)

        def padded(ref, gi):
            return jnp.concatenate([ref[gi], pad_new], axis=0).astype(_MXU_DTYPE)

        def biased(s, valid, dist, slope):
            return jnp.where(valid, s * scale - slope * dist, NEG_INF)

        def two_part(s_p, s_n, v_p, v_n):
            m = jnp.maximum(jnp.max(s_p, axis=-1, keepdims=True), jnp.max(s_n, axis=-1, keepdims=True))
            e_p = jnp.exp(s_p - m)
            e_n = jnp.exp(s_n - m)
            l = jnp.sum(e_p, axis=-1, keepdims=True) + jnp.sum(e_n, axis=-1, keepdims=True)
            return (_dot(e_p.astype(_MXU_DTYPE), v_p) + _dot(e_n.astype(_MXU_DTYPE), v_n)) / l

        qs, slopes, p_cs, imps = [], [], [], []
        cidx = lax.broadcasted_iota(jnp.int32, (1, n_chunk), 1)
        cend = cidx * CMP_STRIDE + (CMP_LEN - 1)
        for gi in range(g):
            q = q_ref[gi].astype(_MXU_DTYPE)
            slope = jnp.zeros((rows, 1), _F32)
            for r in range(rep):
                slope = jnp.where(head_r == r, slope_ref[gi * rep + r], slope)
            kc = kc_all[gi * n_chunk:(gi + 1) * n_chunk].astype(_MXU_DTYPE)
            s = biased(_dot_nt(q, kc), (cend <= qpos) & (cidx < n_cmp), (qpos - cend).astype(_F32), slope)
            e = jnp.exp(s - jnp.max(s, axis=-1, keepdims=True))
            p_c = e / jnp.sum(e, axis=-1, keepdims=True) * (qpos >= CMP_LEN - 1).astype(_F32)
            p3 = p_c.reshape(t_new, rep, n_chunk)
            p_sum = jnp.broadcast_to(jnp.sum(p3, axis=1, keepdims=True), p3.shape).reshape(rows, n_chunk)
            imps.append(_dot_exact(p_sum, ov_ref[...]))
            p_cs.append(p_c.astype(_MXU_DTYPE))
            qs.append(q)
            slopes.append(slope)

        npos = past + lax.broadcasted_iota(jnp.int32, (1, LANES), 1)
        wbuf = cwin_ref.shape[0] // (2 * g)
        wpos = past - wbuf + lax.broadcasted_iota(jnp.int32, (1, wbuf), 1)
        o_wins, s_raw, s_raw_new = [None] * g, [None] * g, [None] * g

        def selected_scores(gi):
            s_raw[gi] = _dot_nt(qs[gi], ksb[gi])
            s_raw_new[gi] = _dot_nt(qs[gi], padded(ksn_ref, gi))

        def window_branch(gi):
            q, slope = qs[gi], slopes[gi]
            kw = cwin_ref[pl.ds(gi, wbuf, stride=2 * g), :].astype(_MXU_DTYPE)
            vw = cwin_ref[pl.ds(g + gi, wbuf, stride=2 * g), :].astype(_MXU_DTYPE)
            dw = qpos - wpos
            s_p = biased(_dot_nt(q, kw), (dw >= 0) & (dw < A_WINDOW), dw.astype(_F32), slope)
            dn = qpos - npos
            s_n = biased(_dot_nt(q, padded(kwn_ref, gi)), (dn >= 0) & (dn < A_WINDOW), dn.astype(_F32), slope)
            o_wins[gi] = two_part(s_p, s_n, vw, padded(vwn_ref, gi))

        fillers = [functools.partial(fn, gi) for gi in range(g) for fn in (selected_scores, window_branch)]
        selm_all = _select_blocks_rows(jnp.concatenate(imps, axis=0), jnp.concatenate([qpos] * g, axis=0), n_sel,
                                       between=fillers)

        kpos = lax.broadcasted_iota(jnp.int32, (1, past), 1)
        new_blk = past // SEL_BLOCK
        for gi in range(g):
            q, slope = qs[gi], slopes[gi]
            selm = selm_all[gi * rows:(gi + 1) * rows]
            o_cmp = _dot(p_cs[gi], vc_all[gi * n_chunk:(gi + 1) * n_chunk].astype(_MXU_DTYPE))
            picked = _dot(selm.astype(_MXU_DTYPE), emat[...])
            s_p = biased(s_raw[gi], (picked > 0.5) & (kpos <= qpos), (qpos - kpos).astype(_F32), slope)
            sel_new = selm[:, new_blk:new_blk + 1] > 0.5
            s_n = biased(s_raw_new[gi], sel_new & (npos <= qpos), (qpos - npos).astype(_F32), slope)
            o_sel = two_part(s_p, s_n, vsb[gi], padded(vsn_ref, gi))

            gt = gt_ref[gi]
            o_ref[gi] = gt[:, 0:1] * o_cmp + gt[:, 1:2] * o_sel + gt[:, 2:3] * o_wins[gi]


def _nsa_sample(page_table, slopes, cache_rows, q_s, gt_s, ksn, vsn, kwn, vwn, cwin, w1cat, pe2, w2, kgain0, ov_s,
                *, past, t_new, page_len, pps=8):
    dbsz, n_pages = page_table.shape
    d = A_HEAD_DIM
    g = A_KV_GROUPS
    rows = t_new * A_REP
    n_chunk = past // CMP_STRIDE
    n_cmp = n_chunk - CMP_RATIO + 1
    n_sel = -(-(past + t_new) // SEL_BLOCK)
    n_blk_l = ov_s.shape[1]
    assert past % CMP_STRIDE == 0 and t_new < CMP_STRIDE and n_pages % pps == 0 and past % SEL_BLOCK == 0
    assert t_new <= SEL_BLOCK and n_sel <= n_blk_l
    page_rows = page_len * 4 * g
    cpp = page_len // CMP_STRIDE

    def page_spec(pi):
        return pl.BlockSpec((None, page_rows, d), lambda b, pp, pt, pi=pi: (pt[b * n_pages + pp * pps + pi], 0, 0))

    per_b = lambda shape: pl.BlockSpec((None,) + shape, lambda b, pp, pt: (b,) + (0,) * len(shape))
    whole = lambda a: pl.BlockSpec(a.shape, lambda b, pp, pt: (0,) * a.ndim)
    grid_spec = pltpu.PrefetchScalarGridSpec(
        num_scalar_prefetch=1,
        grid=(dbsz, n_pages // pps),
        in_specs=[pl.BlockSpec(memory_space=pltpu.SMEM)] + [page_spec(pi) for pi in range(pps)] + [
            per_b((g, rows, d)), per_b((g, rows, 3)),
            per_b(ksn.shape[1:]), per_b(vsn.shape[1:]), per_b(kwn.shape[1:]), per_b(vwn.shape[1:]),
            per_b(cwin.shape[1:]),
            whole(w1cat), whole(pe2), whole(w2), whole(kgain0), whole(ov_s),
        ],
        out_specs=per_b((g, rows, d)),
        scratch_shapes=[
            pltpu.VMEM((2, g * pps, CMP_STRIDE * cpp, d), _F32), pltpu.VMEM((2, g * n_chunk, CMP_RATIO * d), _F32),
            pltpu.VMEM((g, past, d), _MXU_DTYPE), pltpu.VMEM((g, past, d), _MXU_DTYPE),
            pltpu.VMEM((n_blk_l, past), _MXU_DTYPE),
        ],
    )
    return pl.pallas_call(
        functools.partial(_nsa_sample_body, pps=pps, past=past, t_new=t_new, n_cmp=n_cmp, n_sel=n_sel),
        grid_spec=grid_spec,
        out_shape=jax.ShapeDtypeStruct((dbsz, g, rows, d), _F32),
        compiler_params=_cparams(("arbitrary", "arbitrary"), 60),
    )(page_table.reshape(-1), slopes, *([cache_rows] * pps), q_s, gt_s, ksn, vsn, kwn, vwn, cwin,
      w1cat, pe2, w2, kgain0, ov_s)


def _swa_prompt_body(sink_ref, qt_ref, kv_ref, o_ref, kb, vt, *, tq, t_len):
    i = pl.program_id(1)
    hd = B_HEAD_DIM
    nk = B_KV_HEADS * hd
    scale2 = hd ** -0.5 * LOG2E
    slopes = _alibi_slopes(B_HEADS)
    n_wblk = (B_WINDOW + tq) // LANES
    wlen = n_wblk * LANES

    @pl.when(i == 0)
    def _():
        kb[...] = kv_ref[:, :nk].astype(kb.dtype)
        for c in range(t_len // LANES):
            vt[c] = kv_ref[c * LANES:(c + 1) * LANES, nk:].T.astype(vt.dtype)

    t0 = i * tq
    wb0 = jnp.maximum(i * (tq // LANES) - B_WINDOW // LANES, 0)
    start = pl.multiple_of(wb0 * LANES, LANES)
    tpos = t0 + lax.broadcasted_iota(jnp.int32, (1, tq), 1)
    kpos = start + lax.broadcasted_iota(jnp.int32, (wlen, 1), 0)
    dw = tpos - kpos
    distm = jnp.where((dw >= 0) & (dw < B_WINDOW), dw.astype(_F32), MASKED_DIST)
    kwin = kb[pl.ds(start, wlen), :]
    vwin_t = jnp.concatenate([vt[wb0 + j] for j in range(n_wblk)], axis=1)
    qt = qt_ref[...]
    zeros = jnp.zeros((hd, tq), qt.dtype)
    for kh in range(B_KV_HEADS):
        pb = (kh // 2) * LANES
        blocks = []
        for r in range(B_REP):
            h = kh * B_REP + r
            qh = qt[h * hd:(h + 1) * hd, :]
            blocks.append(jnp.concatenate([qh, zeros] if kh % 2 == 0 else [zeros, qh], axis=0))
        s_t = _dot(kwin[:, pb:pb + LANES], jnp.concatenate(blocks, axis=1))
        ps, ls = [], []
        for r in range(B_REP):
            h = kh * B_REP + r
            e, _, l = _softmax_cols(s_t[:, r * tq:(r + 1) * tq], distm, float(slopes[h]) * LOG2E, scale2,
                                    sink2=sink_ref[h] * LOG2E)
            ps.append(e.astype(_MXU_DTYPE))
            ls.append(l)
        o_t = (_dot(vwin_t[kh * hd:(kh + 1) * hd, :], jnp.concatenate(ps, axis=1))
               * (1.0 / jnp.concatenate(ls, axis=1)))
        for p in range(B_REP // 2):
            pair = jnp.concatenate([o_t[:, (2 * p) * tq:(2 * p + 1) * tq], o_t[:, (2 * p + 1) * tq:(2 * p + 2) * tq]], axis=0)
            col0 = (kh * B_REP + 2 * p) * hd
            o_ref[:, col0:col0 + LANES] = pair.T.astype(o_ref.dtype)


def _swa_prompt(qt_all, kv_all, sinks, *, bsz, t_len, tq=256):
    nq = t_len // tq
    hq = B_HEADS * B_HEAD_DIM
    nk = B_KV_HEADS * B_HEAD_DIM
    assert t_len % tq == 0 and t_len >= B_WINDOW + tq and tq % LANES == 0 and B_WINDOW % LANES == 0
    return pl.pallas_call(
        functools.partial(_swa_prompt_body, tq=tq, t_len=t_len),
        grid=(bsz, nq),
        in_specs=[
            pl.BlockSpec(memory_space=pltpu.SMEM),
            pl.BlockSpec((hq, tq), lambda b, i: (0, b * nq + i)),
            pl.BlockSpec((t_len, kv_all.shape[1]), lambda b, i: (b, 0)),
        ],
        out_specs=pl.BlockSpec((tq, hq), lambda b, i: (b * nq + i, 0)),
        out_shape=jax.ShapeDtypeStruct((bsz * t_len, hq), _MXU_DTYPE),
        scratch_shapes=[pltpu.VMEM((t_len, nk), _MXU_DTYPE), pltpu.VMEM((t_len // LANES, nk, LANES), _MXU_DTYPE)],
        compiler_params=_cparams(("arbitrary", "arbitrary"), 40),
    )(sinks, qt_all, kv_all)


def _swa_sample_body(qbd_ref, ckv_ref, nkv_ref, lane_ref, o_ref, *, bt, past, wbuf, tk):
    hd = B_HEAD_DIM
    nk = B_KV_HEADS * hd
    scale2 = hd ** -0.5 * LOG2E
    lanes = qbd_ref.shape[2]
    slope2 = lane_ref[0:1, :]
    sink2 = lane_ref[1:2, :]
    qpos = past + lane_ref[2:3, :].astype(jnp.int32)
    kpos = past - wbuf + lax.broadcasted_iota(jnp.int32, (tk, 1), 0)
    dw = qpos - kpos
    distm = jnp.where((dw >= 0) & (dw < B_WINDOW), dw.astype(_F32), MASKED_DIST)
    lane_kh = _idiv(lax.broadcasted_iota(jnp.int32, (hd, lanes), 1), lanes // B_KV_HEADS)
    pad = jnp.zeros((tk - wbuf - nkv_ref.shape[1], 2 * nk), _F32)
    for bi in range(bt):
        kv = jnp.concatenate([ckv_ref[bi], nkv_ref[bi], pad], axis=0)
        s_t = _dot(kv[:, :nk].astype(_MXU_DTYPE), qbd_ref[bi])
        e, _, l = _softmax_cols(s_t, distm, slope2, scale2, sink2=sink2)
        p_t = (e * (1.0 / l)).astype(_MXU_DTYPE)
        o_full = _dot(kv[:, nk:].T.astype(_MXU_DTYPE), p_t)
        o = jnp.zeros((hd, lanes), _F32)
        for kh in range(B_KV_HEADS):
            o = jnp.where(lane_kh == kh, o_full[kh * hd:(kh + 1) * hd, :], o)
        o_ref[bi] = o


def _swa_sample(qbd, ckv, nkv, lane_consts, *, past, bt=16):
    dbsz, nk, lanes = qbd.shape
    wbuf = ckv.shape[1]
    tk = -(-(wbuf + nkv.shape[1]) // LANES) * LANES
    bt = _pick_tile(dbsz, bt)
    return pl.pallas_call(
        functools.partial(_swa_sample_body, bt=bt, past=past, wbuf=wbuf, tk=tk),
        grid=(dbsz // bt,),
        in_specs=[
            pl.BlockSpec((bt,) + qbd.shape[1:], lambda b: (b, 0, 0)),
            pl.BlockSpec((bt,) + ckv.shape[1:], lambda b: (b, 0, 0)),
            pl.BlockSpec((bt,) + nkv.shape[1:], lambda b: (b, 0, 0)),
            pl.BlockSpec(lane_consts.shape, lambda b: (0, 0)),
        ],
        out_specs=pl.BlockSpec((bt, B_HEAD_DIM, lanes), lambda b: (b, 0, 0)),
        out_shape=jax.ShapeDtypeStruct((dbsz, B_HEAD_DIM, lanes), _F32),
        compiler_params=_cparams(("arbitrary",), 40),
    )(qbd, ckv, nkv, lane_consts)


def _overlap(n_cmp, n_sel, rows, cols):
    start = np.arange(n_cmp) * CMP_STRIDE
    end = start + CMP_LEN - 1
    s0 = np.arange(n_sel) * SEL_BLOCK
    s1 = s0 + SEL_BLOCK - 1
    m = np.zeros((rows, cols), np.float32)
    m[:n_cmp, :n_sel] = (start[:, None] <= s1[None, :]) & (end[:, None] >= s0[None, :])
    return m


def _expand_mat_t(n_blk, t_len, chunk):
    key_blk = np.arange(t_len) // SEL_BLOCK
    m = (key_blk[:, None] == np.arange(n_blk)[None, :]).astype(np.float32)
    return jnp.asarray(m.reshape(t_len // chunk, chunk, n_blk), dtype=_MXU_DTYPE)


def kernel(x_prompt, x_sample, cache_nsa_kv, cache_nsa_win, cache_shared_win, page_table,
           norm_attn, norm_mlp, a_w_in, a_q_gain, a_k_gain, a_cmp_pe, a_cmp_w1, a_cmp_w2, a_w_out,
           kv_norm, kv_w, kv_k_gain, b_w_q, b_q_gain, b_sinks, b_w_out, mlp_w1, mlp_w2):
    bsz, t_len, dm = x_prompt.shape
    dbsz, t_new, _ = x_sample.shape
    n_a = a_w_in.shape[0]
    depth = norm_attn.shape[0]
    n_pool, page_len = cache_nsa_kv.shape[1:3]
    past = page_table.shape[1] * page_len
    d, g, rep = A_HEAD_DIM, A_KV_GROUPS, A_REP
    hq = A_HEADS * d
    hkv = 6 * g * d
    mp = bsz * t_len
    ms = dbsz * t_new
    wdt = _MXU_DTYPE
    slopes_a = jnp.asarray(_alibi_slopes(A_HEADS))

    hs2 = [x_prompt.reshape(mp, dm), x_sample.reshape(ms, dm)]
    both = lambda fn: [fn(hh) for hh in hs2]
    rows_p, rows_s, win_p, win_s = [], [], [], []
    kv_sh = None
    for layer in range(depth):
        if layer < n_a:
            a = layer
            w_in = a_w_in[a]
            wq = w_in[:, :hq].astype(wdt)
            qt_p, qt_s = both(lambda hh: _rms_mm(hh, norm_attn[layer], wq, epi="headnorm", transpose_out=True,
                                                 gain=jnp.tile(a_q_gain[a], A_HEADS), flag=jnp.ones((hq,), _F32),
                                                 out_dtype=wdt))
            one, zero = jnp.ones((g * d,), _F32), jnp.zeros((g * d,), _F32)
            kgain = a_k_gain[a]
            kv_gain = jnp.concatenate([one, one, jnp.tile(kgain[1], g), one, jnp.tile(kgain[2], g), one])
            kv_flag = jnp.concatenate([zero, zero, one, zero, one, zero])
            wkv = w_in[:, hq:hq + hkv].astype(wdt)
            kv_proj = functools.partial(_rms_mm, gamma=norm_attn[layer], w=wkv, epi="headnorm", gain=kv_gain, flag=kv_flag,
                                        split=(4 * g * d, 2 * g * d))
            rw_p, wn_p, rw_il = kv_proj(hs2[0], interleave=4 * g)
            rw_s, wn_s = kv_proj(hs2[1])
            n_gate = 3 * A_HEADS
            w_gate = jnp.pad(w_in[:, hq + hkv:], ((0, 0), (0, LANES - n_gate))).astype(wdt)
            gates_p, gates_s = both(lambda hh: _rms_mm(hh, norm_attn[layer], w_gate, epi="sigmoid")[:, :n_gate])

            w1cat = (a_cmp_w1[a].reshape(2, CMP_RATIO, CMP_STRIDE, d, d).transpose(0, 2, 3, 1, 4)
                     .reshape(2, CMP_STRIDE * d, CMP_RATIO * d).astype(wdt))
            pe2 = jnp.pad(a_cmp_pe[a].reshape(2, CMP_RATIO, CMP_STRIDE * d),
                          ((0, 0), (0, SUBLANES - CMP_RATIO), (0, 0))).astype(wdt)
            w2 = a_cmp_w2[a].astype(wdt)
            kgain0 = kgain[0].reshape(1, d)

            kc, vct = _cmp_prompt(rw_p, w1cat, pe2, w2, kgain0, bsz=bsz, t_len=t_len)
            n_chunk = t_len // CMP_STRIDE
            n_sel_p = -(-t_len // SEL_BLOCK)
            n_blk_p = -(-n_sel_p // SUBLANES) * SUBLANES
            ovt_p = jnp.asarray(_overlap(n_chunk - CMP_RATIO + 1, n_sel_p, n_chunk, n_blk_p).T)
            chunk = 512
            et = _expand_mat_t(n_blk_p, t_len, chunk)
            gates_t = (gates_p.reshape(bsz, t_len, 3, g, rep).transpose(0, 3, 2, 4, 1)
                       .reshape(bsz, g, 3 * rep, t_len))
            o_p = _nsa_prompt(qt_p, rw_p, wn_p, gates_t, kc, vct, ovt_p, et, slopes_a,
                              bsz=bsz, t_len=t_len, chunk=chunk)

            q_s = (qt_s.astype(_F32).reshape(g, rep, d, dbsz, t_new).transpose(3, 0, 4, 1, 2)
                   .reshape(dbsz, g, t_new * rep, d))
            gt_s = (gates_s.reshape(dbsz, t_new, 3, g, rep).transpose(0, 3, 1, 4, 2)
                    .reshape(dbsz, g, t_new * rep, 3))
            kv_s = jnp.concatenate([rw_s, wn_s], axis=1).reshape(dbsz, t_new, 6, g, d)
            new_rows = lambda slot: jnp.pad(kv_s[:, :, slot].transpose(0, 2, 1, 3),
                                            ((0, 0), (0, 0), (0, SUBLANES - t_new), (0, 0)))
            cache_rows = cache_nsa_kv[a].reshape(n_pool, page_len * 4 * g, d)
            cwin = cache_nsa_win[a]
            wbuf = cwin.shape[1]
            n_chunk_s = past // CMP_STRIDE
            n_sel_s = -(-(past + t_new) // SEL_BLOCK)
            n_blk_l = -(-n_sel_s // LANES) * LANES
            ov_s = jnp.asarray(_overlap(n_chunk_s - CMP_RATIO + 1, n_sel_s, n_chunk_s, n_blk_l))
            o_s = _nsa_sample(page_table, slopes_a, cache_rows, q_s, gt_s, new_rows(2), new_rows(3), new_rows(4),
                              new_rows(5), cwin.reshape(dbsz, wbuf * 2 * g, d), w1cat, pe2, w2, kgain0, ov_s,
                              past=past, t_new=t_new, page_len=page_len)
            o_s = o_s.reshape(dbsz, g, t_new, rep, d).transpose(0, 2, 1, 3, 4).reshape(ms, hq).astype(wdt)
            w_out = a_w_out[a]

            rows_p.append(rw_il.reshape(bsz, t_len, 4, g, d))
            rows_s.append(rw_s.reshape(dbsz, t_new, 4, g, d))
            win_p.append(wn_p.reshape(bsz, t_len, 2, g, d)[:, -min(A_WINDOW, t_len):])
            win_s.append(jnp.concatenate([cwin, wn_s.reshape(dbsz, t_new, 2, g, d)], axis=1)[:, -wbuf:])
        else:
            bl = layer - n_a
            hb = B_HEADS * B_HEAD_DIM
            nk = B_KV_HEADS * B_HEAD_DIM
            wq = b_w_q[bl].astype(wdt)
            qt_p, qt_s = both(lambda hh: _rms_mm(hh, norm_attn[layer], wq, epi="headnorm", hd=B_HEAD_DIM,
                                                 transpose_out=True, gain=jnp.tile(b_q_gain[bl], B_HEADS),
                                                 flag=jnp.ones((hb,), _F32), out_dtype=wdt))
            o_p = _swa_prompt(qt_p, kv_sh[0], b_sinks[bl], bsz=bsz, t_len=t_len)

            q5 = (qt_s.astype(_F32).reshape(B_KV_HEADS, B_REP, B_HEAD_DIM, dbsz, t_new)
                  .transpose(3, 0, 2, 1, 4).reshape(dbsz, B_KV_HEADS, B_HEAD_DIM, B_REP * t_new))
            qbd = (q5[:, :, :, None, :] * jnp.eye(B_KV_HEADS, dtype=_F32)[None, :, None, :, None]
                   ).reshape(dbsz, nk, B_HEADS * t_new).astype(wdt)
            slopes_b = jnp.asarray(_alibi_slopes(B_HEADS))
            lane_consts = jnp.stack([jnp.repeat(slopes_b * LOG2E, t_new), jnp.repeat(b_sinks[bl] * LOG2E, t_new),
                                     jnp.tile(jnp.arange(t_new, dtype=_F32), B_HEADS)]
                                    + [jnp.zeros((B_HEADS * t_new,), _F32)] * (SUBLANES - 3))
            nkv = jnp.pad(kv_sh[1].reshape(dbsz, t_new, 2 * nk), ((0, 0), (0, SUBLANES - t_new), (0, 0)))
            ckv = cache_shared_win.reshape(dbsz, cache_shared_win.shape[1], 2 * nk)
            o_s = _swa_sample(qbd, ckv, nkv, lane_consts, past=past)
            o_s = (o_s.reshape(dbsz, B_HEAD_DIM, B_HEADS, t_new).transpose(0, 3, 2, 1).reshape(ms, hb).astype(wdt))
            w_out = b_w_out[bl]

        w_o = w_out.astype(wdt)
        h_s, w_1, w_2 = _mlp_cast(_mm_res(o_s, w_o, hs2[1]), norm_mlp[layer], mlp_w1[layer], mlp_w2[layer])
        hs2 = [_mlp(_mm_res(o_p, w_o, hs2[0]), norm_mlp[layer], w_1, w_2), h_s]
        if layer == n_a - 1:
            nk = B_KV_HEADS * B_HEAD_DIM
            wkv_sh = kv_w.astype(wdt)
            kv_sh = both(lambda hh: _rms_mm(
                hh, kv_norm, wkv_sh, epi="headnorm", hd=B_HEAD_DIM,
                gain=jnp.concatenate([jnp.tile(kv_k_gain, B_KV_HEADS), jnp.ones((nk,), _F32)]),
                flag=jnp.concatenate([jnp.ones((nk,), _F32), jnp.zeros((nk,), _F32)])))

    wb = cache_shared_win.shape[1]
    kv_p = kv_sh[0].reshape(bsz, t_len, 2, B_KV_HEADS, B_HEAD_DIM)
    kv_s_new = kv_sh[1].reshape(dbsz, t_new, 2, B_KV_HEADS, B_HEAD_DIM)
    return (hs2[0].reshape(bsz, t_len, dm), hs2[1].reshape(dbsz, t_new, dm),
            jnp.stack(rows_p), jnp.stack(rows_s), jnp.stack(win_p), jnp.stack(win_s),
            kv_p[:, -min(B_WINDOW, t_len):], jnp.concatenate([cache_shared_win, kv_s_new], axis=1)[:, -wb:])
```

```python
import functools

import jax
import jax.numpy as jnp
import numpy as np
from jax import lax
from jax.experimental import pallas as pl
from jax.experimental.pallas import tpu as pltpu

A_HEADS = 16
A_HEAD_DIM = 128
A_KV_GROUPS = 2
A_REP = A_HEADS // A_KV_GROUPS
CMP_LEN = 32
CMP_STRIDE = 16
CMP_RATIO = CMP_LEN // CMP_STRIDE
SEL_BLOCK = 64
SEL_TOPK = 16
N_LOCAL_FORCED = 2
A_WINDOW = 512
B_HEADS = 32
B_HEAD_DIM = 64
B_KV_HEADS = 4
B_REP = B_HEADS // B_KV_HEADS
B_WINDOW = 128
NORM_EPS = 1e-6
NEG_INF = -1e30
FORCE_BONUS = 1e4

LANES = 128
SUBLANES = 8
X2_PITCH = SUBLANES + 1

LOG2E = float(np.log2(np.e))
MASKED_DIST = 1e33

_MXU_DTYPE = jnp.bfloat16
_F32 = jnp.float32


def _cparams(sem, vmem_mb):
    return pltpu.CompilerParams(dimension_semantics=sem, vmem_limit_bytes=vmem_mb * 1024 * 1024)


def _dot(a, b):
    return jnp.dot(a, b, preferred_element_type=_F32)


def _dot_nt(a, b):
    return lax.dot_general(a, b, (((1,), (1,)), ((), ())), preferred_element_type=_F32)


def _split3(a):
    hi = a.astype(_MXU_DTYPE)
    r1 = a - hi.astype(_F32)
    mid = r1.astype(_MXU_DTYPE)
    lo = (r1 - mid.astype(_F32)).astype(_MXU_DTYPE)
    return hi, mid, lo


def _dot_exact(a, b01):
    bm = b01.astype(_MXU_DTYPE)
    hi, mid, lo = _split3(a)
    return _dot(hi, bm) + _dot(mid, bm) + _dot(lo, bm)


def _dot_exact_rhs(a01, b):
    am = a01.astype(_MXU_DTYPE)
    hi, mid, lo = _split3(b)
    return _dot(am, hi) + _dot(am, mid) + _dot(am, lo)


def _idiv(x, n):
    if n & (n - 1) == 0:
        return lax.shift_right_logical(x, jnp.int32(n.bit_length() - 1))
    return x // n


def _pick_tile(m, want):
    if m <= want:
        return m
    for t in range(want - want % SUBLANES, 0, -SUBLANES):
        if m % t == 0:
            return t
    raise ValueError(f"no tile for {m}")


def _alibi_slopes(n_heads):
    return np.asarray(2.0 ** (-8.0 * np.arange(1, n_heads + 1) / n_heads), dtype=np.float32)


def _rms_rows(x, gain):
    ms = jnp.mean(x * x, axis=-1, keepdims=True)
    return x * lax.rsqrt(ms + NORM_EPS) * gain


def _head_norm(y, gain, flag, hd):
    outs = []
    for c in range(y.shape[1] // LANES):
        yc = y[:, c * LANES:(c + 1) * LANES]
        y2 = yc * yc
        if hd == LANES:
            ms = jnp.mean(y2, axis=-1, keepdims=True)
        else:
            row = _idiv(lax.broadcasted_iota(jnp.int32, (LANES, LANES), 0), hd)
            col = _idiv(lax.broadcasted_iota(jnp.int32, (LANES, LANES), 1), hd)
            ms = _dot_exact(y2, (row == col).astype(_F32)) * (1.0 / hd)
        yn = yc * lax.rsqrt(ms + NORM_EPS) * gain[:, c * LANES:(c + 1) * LANES]
        outs.append(jnp.where(flag[:, c * LANES:(c + 1) * LANES] > 0, yn, yc))
    return outs[0] if len(outs) == 1 else jnp.concatenate(outs, axis=1)


def _rms_mm_body(x_ref, gam_ref, w_ref, gain_ref, flag_ref, *refs, epi, hd, transpose_out, interleave):
    o_refs, xn_ref = refs[:-1], refs[-1]
    if interleave:
        o_refs, il_ref = o_refs[:-1], o_refs[-1]

    @pl.when(pl.program_id(1) == 0)
    def _():
        xn_ref[...] = _rms_rows(x_ref[...], gam_ref[...]).astype(xn_ref.dtype)

    y = _dot(xn_ref[...], w_ref[...])
    if epi == "headnorm":
        y = _head_norm(y, gain_ref[...], flag_ref[...], hd)
    elif epi == "sigmoid":
        y = jax.nn.sigmoid(y)
    if transpose_out:
        y = y.T
    col = 0
    for o_ref in o_refs:
        o_ref[...] = y[:, col:col + o_ref.shape[1]].astype(o_ref.dtype)
        col += o_ref.shape[1]
    if interleave:
        for c in range(interleave):
            il_ref[pl.ds(c, y.shape[0], stride=interleave), :] = y[:, c * LANES:(c + 1) * LANES]


def _rms_mm(x, gamma, w, *, epi="none", gain=None, flag=None, hd=LANES, out_dtype=_F32, transpose_out=False,
            split=None, interleave=0, tm=512, tn=2048):
    m, k = x.shape
    n = w.shape[1]
    tm = _pick_tile(m, tm)
    tn = _pick_tile(n, tn)
    if gain is None:
        gain = jnp.ones((n,), _F32)
        flag = jnp.zeros((n,), _F32)
    if transpose_out:
        out_spec = pl.BlockSpec((tn, tm), lambda i, j: (j, i))
        out_shape = jax.ShapeDtypeStruct((n, m), out_dtype)
    elif split is not None:
        assert tn == n and sum(split) == n
        out_spec = [pl.BlockSpec((tm, ns), lambda i, j: (i, 0)) for ns in split]
        out_shape = [jax.ShapeDtypeStruct((m, ns), out_dtype) for ns in split]
        if interleave:
            out_spec.append(pl.BlockSpec((tm * interleave, LANES), lambda i, j: (i, 0)))
            out_shape.append(jax.ShapeDtypeStruct((m * interleave, LANES), out_dtype))
    else:
        out_spec = pl.BlockSpec((tm, tn), lambda i, j: (i, j))
        out_shape = jax.ShapeDtypeStruct((m, n), out_dtype)
    return pl.pallas_call(
        functools.partial(_rms_mm_body, epi=epi, hd=hd, transpose_out=transpose_out, interleave=interleave),
        grid=(m // tm, n // tn),
        in_specs=[
            pl.BlockSpec((tm, k), lambda i, j: (i, 0)),
            pl.BlockSpec((1, k), lambda i, j: (0, 0)),
            pl.BlockSpec((k, tn), lambda i, j: (0, j)),
            pl.BlockSpec((1, tn), lambda i, j: (0, j)),
            pl.BlockSpec((1, tn), lambda i, j: (0, j)),
        ],
        out_specs=out_spec,
        out_shape=out_shape,
        scratch_shapes=[pltpu.VMEM((tm, k), _MXU_DTYPE)],
        compiler_params=_cparams(("arbitrary", "arbitrary"), 56),
    )(x, gamma.reshape(1, k), w, gain.reshape(1, n), flag.reshape(1, n))


def _mm_res_body(a_ref, w_ref, h_ref, o_ref):
    o_ref[...] = h_ref[...] + _dot(a_ref[...], w_ref[...])


def _mm_res(a, w, h, *, tm=512, tn=2048):
    m, k = a.shape
    n = w.shape[1]
    tm = _pick_tile(m, tm)
    tn = _pick_tile(n, tn)
    return pl.pallas_call(
        _mm_res_body,
        grid=(m // tm, n // tn),
        in_specs=[
            pl.BlockSpec((tm, k), lambda i, j: (i, 0)),
            pl.BlockSpec((k, tn), lambda i, j: (0, j)),
            pl.BlockSpec((tm, tn), lambda i, j: (i, j)),
        ],
        out_specs=pl.BlockSpec((tm, tn), lambda i, j: (i, j)),
        out_shape=jax.ShapeDtypeStruct((m, n), _F32),
        compiler_params=_cparams(("arbitrary", "arbitrary"), 56),
    )(a, w, h)


def _mlp_body(h_ref, gam_ref, w1_ref, w2_ref, o_ref, xn_ref):
    @pl.when(pl.program_id(1) == 0)
    def _():
        h = h_ref[...]
        xn_ref[...] = _rms_rows(h, gam_ref[...]).astype(xn_ref.dtype)
        o_ref[...] = h

    a = jnp.square(jnp.maximum(_dot(xn_ref[...], w1_ref[...]), 0.0))
    o_ref[...] += _dot(a.astype(_MXU_DTYPE), w2_ref[...])


def _mlp_cast_body(h_ref, gam_ref, w1_ref, w2_ref, o_ref, w1b_ref, w2b_ref, xn_ref):
    @pl.when(pl.program_id(0) == 0)
    def _():
        h = h_ref[...]
        xn_ref[...] = _rms_rows(h, gam_ref[...]).astype(xn_ref.dtype)
        o_ref[...] = h

    w1b = w1_ref[...].astype(w1b_ref.dtype)
    w2b = w2_ref[...].astype(w2b_ref.dtype)
    w1b_ref[...] = w1b
    w2b_ref[...] = w2b
    a = jnp.square(jnp.maximum(_dot(xn_ref[...], w1b), 0.0))
    o_ref[...] += _dot(a.astype(_MXU_DTYPE), w2b)


def _mlp_cast(h, gamma, w1, w2, *, tf=512):
    m, d = h.shape
    f = w1.shape[1]
    tf = _pick_tile(f, tf)
    return pl.pallas_call(
        _mlp_cast_body,
        grid=(f // tf,),
        in_specs=[
            pl.BlockSpec((m, d), lambda j: (0, 0)),
            pl.BlockSpec((1, d), lambda j: (0, 0)),
            pl.BlockSpec((d, tf), lambda j: (0, j)),
            pl.BlockSpec((tf, d), lambda j: (j, 0)),
        ],
        out_specs=[pl.BlockSpec((m, d), lambda j: (0, 0)),
                   pl.BlockSpec((d, tf), lambda j: (0, j)),
                   pl.BlockSpec((tf, d), lambda j: (j, 0))],
        out_shape=[jax.ShapeDtypeStruct((m, d), _F32), jax.ShapeDtypeStruct(w1.shape, _MXU_DTYPE),
                   jax.ShapeDtypeStruct(w2.shape, _MXU_DTYPE)],
        scratch_shapes=[pltpu.VMEM((m, d), _MXU_DTYPE)],
        compiler_params=_cparams(("arbitrary",), 56),
    )(h, gamma.reshape(1, d), w1, w2)


def _mlp(h, gamma, w1, w2, *, tm=1024, tf=512):
    m, d = h.shape
    f = w1.shape[1]
    tm = _pick_tile(m, tm)
    tf = _pick_tile(f, tf)
    return pl.pallas_call(
        _mlp_body,
        grid=(m // tm, f // tf),
        in_specs=[
            pl.BlockSpec((tm, d), lambda i, j: (i, 0)),
            pl.BlockSpec((1, d), lambda i, j: (0, 0)),
            pl.BlockSpec((d, tf), lambda i, j: (0, j)),
            pl.BlockSpec((tf, d), lambda i, j: (j, 0)),
        ],
        out_specs=pl.BlockSpec((tm, d), lambda i, j: (i, 0)),
        out_shape=jax.ShapeDtypeStruct((m, d), _F32),
        scratch_shapes=[pltpu.VMEM((tm, d), _MXU_DTYPE)],
        compiler_params=_cparams(("arbitrary", "arbitrary"), 56),
    )(h, gamma.reshape(1, d), w1, w2)


def _block_scores(imp, blk, cur, n_sel):
    score, _ = _block_scores_forced(imp, blk, cur, n_sel)
    return score


def _block_scores_forced(imp, blk, cur, n_sel):
    valid = blk <= cur
    forced = (blk == 0) | (((cur - blk) < N_LOCAL_FORCED) & valid)
    score = jnp.where(valid, imp + FORCE_BONUS * forced.astype(_F32), NEG_INF)
    return jnp.where(blk < n_sel, score, -jnp.inf), forced


def _select_blocks_rows(imp, qpos, n_sel, between=()):
    assert FORCE_BONUS > 2 * A_REP
    n_forced = 1 + N_LOCAL_FORCED
    blk = lax.broadcasted_iota(jnp.int32, imp.shape, 1)
    s, forced = _block_scores_forced(imp, blk, _idiv(qpos, SEL_BLOCK), n_sel)
    sel = jnp.where(forced, 1.0, 0.0)
    s = jnp.where(forced, -jnp.inf, s)
    pending = list(between)
    for _ in range(min(SEL_TOPK, n_sel) - n_forced):
        hit = blk == jnp.argmax(s, axis=-1, keepdims=True).astype(jnp.int32)
        sel = jnp.where(hit, 1.0, sel)
        s = jnp.where(hit, -jnp.inf, s)
        if pending:
            pending.pop(0)()
    for work in pending:
        work()
    return sel


def _select_blocks_cols(imp_t, qpos, n_sel):
    blk = lax.broadcasted_iota(jnp.int32, imp_t.shape, 0)
    s = _block_scores(imp_t, blk, _idiv(qpos, SEL_BLOCK), n_sel)
    beaten = jnp.zeros(imp_t.shape, _F32)
    for j in range(n_sel):
        sj = s[j:j + 1, :]
        tie = jnp.where(blk > j, 1.0, 0.0)
        beaten = beaten + jnp.where(sj > s, 1.0, 0.0) + jnp.where(sj == s, tie, 0.0)
    return jnp.where((beaten < min(SEL_TOPK, n_sel)) & (blk < n_sel), 1.0, 0.0)


def _compress_x2(x2, w1cat, pe2, w2, n_chunk):
    return _compress_finish(_dot(x2.astype(_MXU_DTYPE), w1cat), w1cat, pe2, w2, n_chunk)


def _compress_finish(h, w1cat, pe2, w2, n_chunk):
    d = A_HEAD_DIM
    c = _dot(pe2, w1cat)
    part0 = h[:, :d] + c[0:1, :d]
    part1 = h[:, d:] + c[1:2, d:]
    pieces = []
    for s in range(h.shape[0] // n_chunk):
        p1 = part1[s * n_chunk:(s + 1) * n_chunk]
        pieces.append(part0[s * n_chunk:(s + 1) * n_chunk] + pltpu.roll(p1, n_chunk - 1, 0))
    hh = pieces[0] if len(pieces) == 1 else jnp.concatenate(pieces, axis=0)
    return _dot(jax.nn.gelu(hh).astype(_MXU_DTYPE), w2)


def _softmax_cols(s_t, distm, slope2, scale2, sink2=None):
    sc = s_t * scale2 - slope2 * distm
    m = jnp.max(sc, axis=0, keepdims=True)
    if sink2 is not None:
        m = jnp.maximum(m, sink2)
    e = jnp.exp2(sc - m)
    l = jnp.sum(e, axis=0, keepdims=True)
    if sink2 is not None:
        l = l + jnp.exp2(sink2 - m)
    return e, m, l


def _cmp_prompt_body(kraw_ref, vraw_ref, w1_ref, pe_ref, w2_ref, kg_ref, kc_ref, vct_ref, *, n_chunk):
    for slot, src in enumerate((kraw_ref, vraw_ref)):
        x2 = jnp.concatenate([src[pl.ds(s, n_chunk, stride=CMP_STRIDE), :] for s in range(CMP_STRIDE)], axis=1)
        y = _compress_x2(x2, w1_ref[slot], pe_ref[slot], w2_ref[slot], n_chunk)
        if slot == 0:
            kc_ref[...] = _rms_rows(y, kg_ref[...])
        else:
            vct_ref[...] = y.T


def _cmp_prompt(kv, w1cat, pe2, w2, kgain0, *, bsz, t_len):
    n_chunk = t_len // CMP_STRIDE
    d = A_HEAD_DIM
    g = A_KV_GROUPS
    return pl.pallas_call(
        functools.partial(_cmp_prompt_body, n_chunk=n_chunk),
        grid=(bsz, g),
        in_specs=[
            pl.BlockSpec((t_len, d), lambda b, gi: (b, gi)),
            pl.BlockSpec((t_len, d), lambda b, gi: (b, g + gi)),
            pl.BlockSpec(w1cat.shape, lambda b, gi: (0, 0, 0)),
            pl.BlockSpec(pe2.shape, lambda b, gi: (0, 0, 0)),
            pl.BlockSpec(w2.shape, lambda b, gi: (0, 0, 0)),
            pl.BlockSpec((1, d), lambda b, gi: (0, 0)),
        ],
        out_specs=[pl.BlockSpec((None, None, n_chunk, d), lambda b, gi: (b, gi, 0, 0)),
                   pl.BlockSpec((None, None, d, n_chunk), lambda b, gi: (b, gi, 0, 0))],
        out_shape=[jax.ShapeDtypeStruct((bsz, g, n_chunk, d), _F32), jax.ShapeDtypeStruct((bsz, g, d, n_chunk), _F32)],
        compiler_params=_cparams(("arbitrary", "arbitrary"), 32),
    )(kv, kv, w1cat, pe2, w2, kgain0)


def _nsa_prompt_body(slope_ref, qt_ref, gt_ref, kc_ref, vct_ref, ks_ref, vs_ref, kw_ref, vw_ref, ovt_ref, et_ref,
                     o_ref, ksb, vst, kwb, vwt, *, tq, t_len, n_cmp, n_sel, chunk):
    gi = pl.program_id(1)
    i = pl.program_id(2)
    d = A_HEAD_DIM
    rep = A_REP
    scale2 = d ** -0.5 * LOG2E
    n_chunks = t_len // chunk
    n_wblk = A_WINDOW // tq + 1

    @pl.when(i == 0)
    def _():
        ksb[...] = ks_ref[...].astype(ksb.dtype)
        kwb[...] = kw_ref[...].astype(kwb.dtype)
        for c in range(n_chunks):
            vst[c] = vs_ref[c * chunk:(c + 1) * chunk, :].T.astype(vst.dtype)
        for c in range(t_len // tq):
            vwt[c] = vw_ref[c * tq:(c + 1) * tq, :].T.astype(vwt.dtype)

    t0 = i * tq
    qt = qt_ref[...]
    qst = jnp.concatenate([qt[r * d:(r + 1) * d, :] for r in range(rep)], axis=1)
    tpos = t0 + lax.broadcasted_iota(jnp.int32, (1, tq), 1)
    slope2 = [slope_ref[gi * rep + r] * LOG2E for r in range(rep)]
    head = lambda a, r: a[:, r * tq:(r + 1) * tq]

    def masked_dist(valid, dist):
        return jnp.where(valid, dist.astype(_F32), MASKED_DIST)

    n_c = kc_ref.shape[0]
    cidx = lax.broadcasted_iota(jnp.int32, (n_c, 1), 0)
    cend = cidx * CMP_STRIDE + (CMP_LEN - 1)
    distm = masked_dist((cend <= tpos) & (cidx < n_cmp), tpos - cend)
    s_t = _dot(kc_ref[...].astype(_MXU_DTYPE), qst)
    any_valid = (tpos >= CMP_LEN - 1).astype(_F32)
    p_heads = []
    for r in range(rep):
        e, _, l = _softmax_cols(head(s_t, r), distm, slope2[r], scale2)
        p_heads.append(e * (any_valid / l))
    p_sum = p_heads[0]
    for r in range(1, rep):
        p_sum = p_sum + p_heads[r]
    o_cmp = _dot(vct_ref[...].astype(_MXU_DTYPE), jnp.concatenate(p_heads, axis=1).astype(_MXU_DTYPE))
    imp_t = _dot_exact_rhs(ovt_ref[...], p_sum)
    sel_t = _select_blocks_cols(imp_t, tpos, n_sel).astype(_MXU_DTYPE)

    n_live = (t0 + tq + chunk - 1) // chunk

    def sel_step(c, carry):
        m_i, l_i, acc = carry
        base = pl.multiple_of(c * chunk, chunk)
        kpos = base + lax.broadcasted_iota(jnp.int32, (chunk, 1), 0)
        picked = _dot(et_ref[c], sel_t)
        distm = masked_dist((picked > 0.5) & (kpos <= tpos), tpos - kpos)
        s_t = _dot(ksb[pl.ds(base, chunk), :], qst)
        es, ms, ls = [], [], []
        for r in range(rep):
            sc = head(s_t, r) * scale2 - slope2[r] * distm
            m_new = jnp.maximum(head(m_i, r), jnp.max(sc, axis=0, keepdims=True))
            e = jnp.exp2(sc - m_new)
            es.append(e.astype(_MXU_DTYPE))
            ms.append(m_new)
            ls.append(jnp.sum(e, axis=0, keepdims=True))
        m_new = jnp.concatenate(ms, axis=1)
        alpha = jnp.exp2(m_i - m_new)
        l_new = alpha * l_i + jnp.concatenate(ls, axis=1)
        acc = alpha * acc + _dot(vst[c], jnp.concatenate(es, axis=1))
        return m_new, l_new, acc

    init = (jnp.full((1, rep * tq), NEG_INF, _F32), jnp.zeros((1, rep * tq), _F32), jnp.zeros((d, rep * tq), _F32))
    _, l_s, acc_s = lax.fori_loop(0, n_live, sel_step, init)
    o_sel = acc_s * (1.0 / l_s)

    wlen = n_wblk * tq
    wb0 = jnp.maximum(i - (n_wblk - 1), 0)
    start = pl.multiple_of(wb0 * tq, tq)
    kpos = start + lax.broadcasted_iota(jnp.int32, (wlen, 1), 0)
    dw = tpos - kpos
    distm = masked_dist((dw >= 0) & (dw < A_WINDOW), dw)
    s_t = _dot(kwb[pl.ds(start, wlen), :], qst)
    es, ls = [], []
    for r in range(rep):
        e, _, l = _softmax_cols(head(s_t, r), distm, slope2[r], scale2)
        es.append(e.astype(_MXU_DTYPE))
        ls.append(l)
    vw = jnp.concatenate([vwt[wb0 + j] for j in range(n_wblk)], axis=1)
    o_win = _dot(vw, jnp.concatenate(es, axis=1)) * (1.0 / jnp.concatenate(ls, axis=1))

    gt = gt_ref[...]
    for r in range(rep):
        o_r = (gt[r:r + 1] * head(o_cmp, r) + gt[rep + r:rep + r + 1] * head(o_sel, r)
               + gt[2 * rep + r:2 * rep + r + 1] * head(o_win, r))
        o_ref[:, r * d:(r + 1) * d] = o_r.T.astype(o_ref.dtype)


def _nsa_prompt(qt_all, rows, win, gates_t, kc, vct, ovt, et, slopes, *, bsz, t_len, tq=256, chunk=512):
    d = A_HEAD_DIM
    g = A_KV_GROUPS
    rep = A_REP
    nq = t_len // tq
    n_chunk = kc.shape[2]
    n_cmp = n_chunk - CMP_RATIO + 1
    n_sel = -(-t_len // SEL_BLOCK)
    assert t_len % tq == 0 and A_WINDOW % tq == 0 and t_len >= A_WINDOW + tq
    assert t_len % chunk == 0 and chunk % SEL_BLOCK == 0
    kvspec = lambda col: pl.BlockSpec((t_len, d), lambda b, gi, i, col=col: (b, col + gi))
    return pl.pallas_call(
        functools.partial(_nsa_prompt_body, tq=tq, t_len=t_len, n_cmp=n_cmp, n_sel=n_sel, chunk=chunk),
        grid=(bsz, g, nq),
        in_specs=[
            pl.BlockSpec(memory_space=pltpu.SMEM),
            pl.BlockSpec((rep * d, tq), lambda b, gi, i: (gi, b * nq + i)),
            pl.BlockSpec((None, None, 3 * rep, tq), lambda b, gi, i: (b, gi, 0, i)),
            pl.BlockSpec((None, None, n_chunk, d), lambda b, gi, i: (b, gi, 0, 0)),
            pl.BlockSpec((None, None, d, n_chunk), lambda b, gi, i: (b, gi, 0, 0)),
            kvspec(2 * g), kvspec(3 * g), kvspec(0), kvspec(g),
            pl.BlockSpec(ovt.shape, lambda b, gi, i: (0, 0)),
            pl.BlockSpec(et.shape, lambda b, gi, i: (0, 0, 0)),
        ],
        out_specs=pl.BlockSpec((tq, rep * d), lambda b, gi, i: (b * nq + i, gi)),
        out_shape=jax.ShapeDtypeStruct((bsz * t_len, g * rep * d), _MXU_DTYPE),
        scratch_shapes=[pltpu.VMEM((t_len, d), _MXU_DTYPE), pltpu.VMEM((t_len // chunk, d, chunk), _MXU_DTYPE),
                        pltpu.VMEM((t_len, d), _MXU_DTYPE), pltpu.VMEM((t_len // tq, d, tq), _MXU_DTYPE)],
        compiler_params=_cparams(("arbitrary", "arbitrary", "arbitrary"), 48),
    )(slopes, qt_all, gates_t, kc, vct, rows, rows, win, win, ovt, et)


def _nsa_sample_body(pt_ref, slope_ref, *refs, pps, past, t_new, n_cmp, n_sel):
    pages = refs[:pps]
    (q_ref, gt_ref, ksn_ref, vsn_ref, kwn_ref, vwn_ref, cwin_ref, w1_ref, pe_ref, w2_ref, kg_ref, ov_ref,
     o_ref, x2, hcmp, ksb, vsb, emat) = refs[pps:]
    b = pl.program_id(0)
    pp = pl.program_id(1)
    n_pp = pl.num_programs(1)
    d = A_HEAD_DIM
    g = A_KV_GROUPS
    rep = A_REP
    rows = t_new * rep
    scale = d ** -0.5
    sg_stride = 4 * g
    page_len = pages[0].shape[0] // sg_stride
    cpp = page_len // CMP_STRIDE
    assert cpp == SUBLANES
    n_chunk = past // CMP_STRIDE
    n_rb = n_chunk // cpp

    @pl.when((b == 0) & (pp == 0))
    def _():
        blk = lax.broadcasted_iota(jnp.int32, emat.shape, 0)
        key = _idiv(lax.broadcasted_iota(jnp.int32, emat.shape, 1), SEL_BLOCK)
        emat[...] = (blk == key).astype(emat.dtype)
        x2[...] = jnp.zeros(x2.shape, x2.dtype)

    step_chunks = pps * cpp
    par = pp % 2

    def step_matmul(buf, step):
        for slot in range(2):
            slabs = [jnp.concatenate([x2[buf, slot, rb, pl.ds(s * X2_PITCH, cpp), :] for s in range(CMP_STRIDE)], axis=1)
                     for rb in range(g * pps)]
            h_step = _dot(jnp.concatenate(slabs, axis=0).astype(_MXU_DTYPE), w1_ref[slot])
            for gi in range(g):
                hrow = pl.multiple_of(gi * n_chunk + step * step_chunks, step_chunks)
                hcmp[slot, pl.ds(hrow, step_chunks), :] = h_step[gi * step_chunks:(gi + 1) * step_chunks]

    step_matmul(1 - par, (pp + n_pp - 1) % n_pp)
    for pi in range(pps):
        pg = pages[pi]
        page_no = pp * pps + pi
        for gi in range(g):
            for slot in range(2):
                a = pg[pl.ds(slot * g + gi, page_len, stride=sg_stride), :]
                for c in range(cpp):
                    for hs in range(CMP_STRIDE // SUBLANES):
                        p0 = c * CMP_STRIDE + hs * SUBLANES
                        x2[par, slot, gi * pps + pi, pl.ds(hs * SUBLANES * X2_PITCH + c, SUBLANES, stride=X2_PITCH), :] = (
                            a[p0:p0 + SUBLANES])
            krow = pl.multiple_of(page_no * page_len, page_len)
            ksb[gi, pl.ds(krow, page_len), :] = pg[pl.ds(2 * g + gi, page_len, stride=sg_stride), :].astype(ksb.dtype)
            vsb[gi, pl.ds(krow, page_len), :] = pg[pl.ds(3 * g + gi, page_len, stride=sg_stride), :].astype(vsb.dtype)

    @pl.when(pp == n_pp - 1)
    def _():
        step_matmul(par, pp)
        kc_all = _rms_rows(_compress_finish(hcmp[0], w1_ref[0], pe_ref[0], w2_ref[0], n_chunk), kg_ref[...])
        vc_all = _compress_finish(hcmp[1], w1_ref[1], pe_ref[1], w2_ref[1], n_chunk)
        rowi = lax.broadcasted_iota(jnp.int32, (rows, 1), 0)
        qpos = past + _idiv(rowi, rep)
        head_r = rowi - _idiv(rowi, rep) * rep
        pad_new = jnp.zeros((LANES - ksn_ref.shape[1], d), _F32)

        def padded(ref, gi):
            return jnp.concatenate([ref[gi], pad_new], axis=0).astype(_MXU_DTYPE)

        def biased(s, valid, dist, slope):
            return jnp.where(valid, s * scale - slope * dist, NEG_INF)

        def two_part(s_p, s_n, v_p, v_n):
            m = jnp.maximum(jnp.max(s_p, axis=-1, keepdims=True), jnp.max(s_n, axis=-1, keepdims=True))
            e_p = jnp.exp(s_p - m)
            e_n = jnp.exp(s_n - m)
            l = jnp.sum(e_p, axis=-1, keepdims=True) + jnp.sum(e_n, axis=-1, keepdims=True)
            return (_dot(e_p.astype(_MXU_DTYPE), v_p) + _dot(e_n.astype(_MXU_DTYPE), v_n)) / l

        qs, slopes, p_cs, imps = [], [], [], []
        cidx = lax.broadcasted_iota(jnp.int32, (1, n_chunk), 1)
        cend = cidx * CMP_STRIDE + (CMP_LEN - 1)
        for gi in range(g):
            q = q_ref[gi].astype(_MXU_DTYPE)
            slope = jnp.zeros((rows, 1), _F32)
            for r in range(rep):
                slope = jnp.where(head_r == r, slope_ref[gi * rep + r], slope)
            kc = kc_all[gi * n_chunk:(gi + 1) * n_chunk].astype(_MXU_DTYPE)
            s = biased(_dot_nt(q, kc), (cend <= qpos) & (cidx < n_cmp), (qpos - cend).astype(_F32), slope)
            e = jnp.exp(s - jnp.max(s, axis=-1, keepdims=True))
            p_c = e / jnp.sum(e, axis=-1, keepdims=True) * (qpos >= CMP_LEN - 1).astype(_F32)
            p3 = p_c.reshape(t_new, rep, n_chunk)
            p_sum = jnp.broadcast_to(jnp.sum(p3, axis=1, keepdims=True), p3.shape).reshape(rows, n_chunk)
            imps.append(_dot_exact(p_sum, ov_ref[...]))
            p_cs.append(p_c.astype(_MXU_DTYPE))
            qs.append(q)
            slopes.append(slope)

        npos = past + lax.broadcasted_iota(jnp.int32, (1, LANES), 1)
        wbuf = cwin_ref.shape[0] // (2 * g)
        wpos = past - wbuf + lax.broadcasted_iota(jnp.int32, (1, wbuf), 1)
        o_wins, s_raw, s_raw_new = [None] * g, [None] * g, [None] * g

        def selected_scores(gi):
            s_raw[gi] = _dot_nt(qs[gi], ksb[gi])
            s_raw_new[gi] = _dot_nt(qs[gi], padded(ksn_ref, gi))

        def window_branch(gi):
            q, slope = qs[gi], slopes[gi]
            kw = cwin_ref[pl.ds(gi, wbuf, stride=2 * g), :].astype(_MXU_DTYPE)
            vw = cwin_ref[pl.ds(g + gi, wbuf, stride=2 * g), :].astype(_MXU_DTYPE)
            dw = qpos - wpos
            s_p = biased(_dot_nt(q, kw), (dw >= 0) & (dw < A_WINDOW), dw.astype(_F32), slope)
            dn = qpos - npos
            s_n = biased(_dot_nt(q, padded(kwn_ref, gi)), (dn >= 0) & (dn < A_WINDOW), dn.astype(_F32), slope)
            o_wins[gi] = two_part(s_p, s_n, vw, padded(vwn_ref, gi))

        fillers = [functools.partial(fn, gi) for gi in range(g) for fn in (selected_scores, window_branch)]
        selm_all = _select_blocks_rows(jnp.concatenate(imps, axis=0), jnp.concatenate([qpos] * g, axis=0), n_sel,
                                       between=fillers)

        kpos = lax.broadcasted_iota(jnp.int32, (1, past), 1)
        new_blk = past // SEL_BLOCK
        for gi in range(g):
            q, slope = qs[gi], slopes[gi]
            selm = selm_all[gi * rows:(gi + 1) * rows]
            o_cmp = _dot(p_cs[gi], vc_all[gi * n_chunk:(gi + 1) * n_chunk].astype(_MXU_DTYPE))
            picked = _dot(selm.astype(_MXU_DTYPE), emat[...])
            s_p = biased(s_raw[gi], (picked > 0.5) & (kpos <= qpos), (qpos - kpos).astype(_F32), slope)
            sel_new = selm[:, new_blk:new_blk + 1] > 0.5
            s_n = biased(s_raw_new[gi], sel_new & (npos <= qpos), (qpos - npos).astype(_F32), slope)
            o_sel = two_part(s_p, s_n, vsb[gi], padded(vsn_ref, gi))

            gt = gt_ref[gi]
            o_ref[gi] = gt[:, 0:1] * o_cmp + gt[:, 1:2] * o_sel + gt[:, 2:3] * o_wins[gi]


def _nsa_sample(page_table, slopes, cache_rows, q_s, gt_s, ksn, vsn, kwn, vwn, cwin, w1cat, pe2, w2, kgain0, ov_s,
                *, past, t_new, page_len, pps=8):
    dbsz, n_pages = page_table.shape
    d = A_HEAD_DIM
    g = A_KV_GROUPS
    rows = t_new * A_REP
    n_chunk = past // CMP_STRIDE
    n_cmp = n_chunk - CMP_RATIO + 1
    n_sel = -(-(past + t_new) // SEL_BLOCK)
    n_blk_l = ov_s.shape[1]
    assert past % CMP_STRIDE == 0 and t_new < CMP_STRIDE and n_pages % pps == 0 and past % SEL_BLOCK == 0
    assert t_new <= SEL_BLOCK and n_sel <= n_blk_l
    assert past // SEL_BLOCK >= N_LOCAL_FORCED and min(SEL_TOPK, n_sel) >= 1 + N_LOCAL_FORCED
    page_rows = page_len * 4 * g
    cpp = page_len // CMP_STRIDE

    def page_spec(pi):
        return pl.BlockSpec((None, page_rows, d), lambda b, pp, pt, pi=pi: (pt[b * n_pages + pp * pps + pi], 0, 0))

    per_b = lambda shape: pl.BlockSpec((None,) + shape, lambda b, pp, pt: (b,) + (0,) * len(shape))
    whole = lambda a: pl.BlockSpec(a.shape, lambda b, pp, pt: (0,) * a.ndim)
    grid_spec = pltpu.PrefetchScalarGridSpec(
        num_scalar_prefetch=1,
        grid=(dbsz, n_pages // pps),
        in_specs=[pl.BlockSpec(memory_space=pltpu.SMEM)] + [page_spec(pi) for pi in range(pps)] + [
            per_b((g, rows, d)), per_b((g, rows, 3)),
            per_b(ksn.shape[1:]), per_b(vsn.shape[1:]), per_b(kwn.shape[1:]), per_b(vwn.shape[1:]),
            per_b(cwin.shape[1:]),
            whole(w1cat), whole(pe2), whole(w2), whole(kgain0), whole(ov_s),
        ],
        out_specs=per_b((g, rows, d)),
        scratch_shapes=[
            pltpu.VMEM((2, 2, g * pps, CMP_STRIDE * X2_PITCH, d), _F32), pltpu.VMEM((2, g * n_chunk, CMP_RATIO * d), _F32),
            pltpu.VMEM((g, past, d), _MXU_DTYPE), pltpu.VMEM((g, past, d), _MXU_DTYPE),
            pltpu.VMEM((n_blk_l, past), _MXU_DTYPE),
        ],
    )
    return pl.pallas_call(
        functools.partial(_nsa_sample_body, pps=pps, past=past, t_new=t_new, n_cmp=n_cmp, n_sel=n_sel),
        grid_spec=grid_spec,
        out_shape=jax.ShapeDtypeStruct((dbsz, g, rows, d), _F32),
        compiler_params=_cparams(("arbitrary", "arbitrary"), 60),
    )(page_table.reshape(-1), slopes, *([cache_rows] * pps), q_s, gt_s, ksn, vsn, kwn, vwn, cwin,
      w1cat, pe2, w2, kgain0, ov_s)


def _swa_prompt_body(sink_ref, qt_ref, kv_ref, o_ref, kb, vt, *, tq, t_len):
    i = pl.program_id(1)
    hd = B_HEAD_DIM
    nk = B_KV_HEADS * hd
    scale2 = hd ** -0.5 * LOG2E
    slopes = _alibi_slopes(B_HEADS)
    n_wblk = (B_WINDOW + tq) // LANES
    wlen = n_wblk * LANES

    @pl.when(i == 0)
    def _():
        kb[...] = kv_ref[:, :nk].astype(kb.dtype)
        for c in range(t_len // LANES):
            vt[c] = kv_ref[c * LANES:(c + 1) * LANES, nk:].T.astype(vt.dtype)

    t0 = i * tq
    wb0 = jnp.maximum(i * (tq // LANES) - B_WINDOW // LANES, 0)
    start = pl.multiple_of(wb0 * LANES, LANES)
    tpos = t0 + lax.broadcasted_iota(jnp.int32, (1, tq), 1)
    kpos = start + lax.broadcasted_iota(jnp.int32, (wlen, 1), 0)
    dw = tpos - kpos
    distm = jnp.where((dw >= 0) & (dw < B_WINDOW), dw.astype(_F32), MASKED_DIST)
    kwin = kb[pl.ds(start, wlen), :]
    vwin_t = jnp.concatenate([vt[wb0 + j] for j in range(n_wblk)], axis=1)
    qt = qt_ref[...]
    zeros = jnp.zeros((hd, tq), qt.dtype)
    for kh in range(B_KV_HEADS):
        pb = (kh // 2) * LANES
        blocks = []
        for r in range(B_REP):
            h = kh * B_REP + r
            qh = qt[h * hd:(h + 1) * hd, :]
            blocks.append(jnp.concatenate([qh, zeros] if kh % 2 == 0 else [zeros, qh], axis=0))
        s_t = _dot(kwin[:, pb:pb + LANES], jnp.concatenate(blocks, axis=1))
        ps, ls = [], []
        for r in range(B_REP):
            h = kh * B_REP + r
            e, _, l = _softmax_cols(s_t[:, r * tq:(r + 1) * tq], distm, float(slopes[h]) * LOG2E, scale2,
                                    sink2=sink_ref[h] * LOG2E)
            ps.append(e.astype(_MXU_DTYPE))
            ls.append(l)
        o_t = (_dot(vwin_t[kh * hd:(kh + 1) * hd, :], jnp.concatenate(ps, axis=1))
               * (1.0 / jnp.concatenate(ls, axis=1)))
        for p in range(B_REP // 2):
            pair = jnp.concatenate([o_t[:, (2 * p) * tq:(2 * p + 1) * tq], o_t[:, (2 * p + 1) * tq:(2 * p + 2) * tq]], axis=0)
            col0 = (kh * B_REP + 2 * p) * hd
            o_ref[:, col0:col0 + LANES] = pair.T.astype(o_ref.dtype)


def _swa_prompt(qt_all, kv_all, sinks, *, bsz, t_len, tq=256):
    nq = t_len // tq
    hq = B_HEADS * B_HEAD_DIM
    nk = B_KV_HEADS * B_HEAD_DIM
    assert t_len % tq == 0 and t_len >= B_WINDOW + tq and tq % LANES == 0 and B_WINDOW % LANES == 0
    return pl.pallas_call(
        functools.partial(_swa_prompt_body, tq=tq, t_len=t_len),
        grid=(bsz, nq),
        in_specs=[
            pl.BlockSpec(memory_space=pltpu.SMEM),
            pl.BlockSpec((hq, tq), lambda b, i: (0, b * nq + i)),
            pl.BlockSpec((t_len, kv_all.shape[1]), lambda b, i: (b, 0)),
        ],
        out_specs=pl.BlockSpec((tq, hq), lambda b, i: (b * nq + i, 0)),
        out_shape=jax.ShapeDtypeStruct((bsz * t_len, hq), _MXU_DTYPE),
        scratch_shapes=[pltpu.VMEM((t_len, nk), _MXU_DTYPE), pltpu.VMEM((t_len // LANES, nk, LANES), _MXU_DTYPE)],
        compiler_params=_cparams(("arbitrary", "arbitrary"), 40),
    )(sinks, qt_all, kv_all)


def _swa_sample_body(qbd_ref, ckv_ref, nkv_ref, lane_ref, o_ref, *, bt, past, wbuf, tk):
    hd = B_HEAD_DIM
    nk = B_KV_HEADS * hd
    scale2 = hd ** -0.5 * LOG2E
    lanes = qbd_ref.shape[2]
    slope2 = lane_ref[0:1, :]
    sink2 = lane_ref[1:2, :]
    qpos = past + lane_ref[2:3, :].astype(jnp.int32)
    kpos = past - wbuf + lax.broadcasted_iota(jnp.int32, (tk, 1), 0)
    dw = qpos - kpos
    distm = jnp.where((dw >= 0) & (dw < B_WINDOW), dw.astype(_F32), MASKED_DIST)
    lane_kh = _idiv(lax.broadcasted_iota(jnp.int32, (hd, lanes), 1), lanes // B_KV_HEADS)
    pad = jnp.zeros((tk - wbuf - nkv_ref.shape[1], 2 * nk), _F32)
    for bi in range(bt):
        kv = jnp.concatenate([ckv_ref[bi], nkv_ref[bi], pad], axis=0)
        s_t = _dot(kv[:, :nk].astype(_MXU_DTYPE), qbd_ref[bi])
        e, _, l = _softmax_cols(s_t, distm, slope2, scale2, sink2=sink2)
        p_t = (e * (1.0 / l)).astype(_MXU_DTYPE)
        o_full = _dot(kv[:, nk:].T.astype(_MXU_DTYPE), p_t)
        o = jnp.zeros((hd, lanes), _F32)
        for kh in range(B_KV_HEADS):
            o = jnp.where(lane_kh == kh, o_full[kh * hd:(kh + 1) * hd, :], o)
        o_ref[bi] = o


def _swa_sample(qbd, ckv, nkv, lane_consts, *, past, bt=16):
    dbsz, nk, lanes = qbd.shape
    wbuf = ckv.shape[1]
    tk = -(-(wbuf + nkv.shape[1]) // LANES) * LANES
    bt = _pick_tile(dbsz, bt)
    return pl.pallas_call(
        functools.partial(_swa_sample_body, bt=bt, past=past, wbuf=wbuf, tk=tk),
        grid=(dbsz // bt,),
        in_specs=[
            pl.BlockSpec((bt,) + qbd.shape[1:], lambda b: (b, 0, 0)),
            pl.BlockSpec((bt,) + ckv.shape[1:], lambda b: (b, 0, 0)),
            pl.BlockSpec((bt,) + nkv.shape[1:], lambda b: (b, 0, 0)),
            pl.BlockSpec(lane_consts.shape, lambda b: (0, 0)),
        ],
        out_specs=pl.BlockSpec((bt, B_HEAD_DIM, lanes), lambda b: (b, 0, 0)),
        out_shape=jax.ShapeDtypeStruct((dbsz, B_HEAD_DIM, lanes), _F32),
        compiler_params=_cparams(("arbitrary",), 40),
    )(qbd, ckv, nkv, lane_consts)


def _overlap(n_cmp, n_sel, rows, cols):
    start = np.arange(n_cmp) * CMP_STRIDE
    end = start + CMP_LEN - 1
    s0 = np.arange(n_sel) * SEL_BLOCK
    s1 = s0 + SEL_BLOCK - 1
    m = np.zeros((rows, cols), np.float32)
    m[:n_cmp, :n_sel] = (start[:, None] <= s1[None, :]) & (end[:, None] >= s0[None, :])
    return m


def _expand_mat_t(n_blk, t_len, chunk):
    key_blk = np.arange(t_len) // SEL_BLOCK
    m = (key_blk[:, None] == np.arange(n_blk)[None, :]).astype(np.float32)
    return jnp.asarray(m.reshape(t_len // chunk, chunk, n_blk), dtype=_MXU_DTYPE)


def kernel(x_prompt, x_sample, cache_nsa_kv, cache_nsa_win, cache_shared_win, page_table,
           norm_attn, norm_mlp, a_w_in, a_q_gain, a_k_gain, a_cmp_pe, a_cmp_w1, a_cmp_w2, a_w_out,
           kv_norm, kv_w, kv_k_gain, b_w_q, b_q_gain, b_sinks, b_w_out, mlp_w1, mlp_w2):
    bsz, t_len, dm = x_prompt.shape
    dbsz, t_new, _ = x_sample.shape
    n_a = a_w_in.shape[0]
    depth = norm_attn.shape[0]
    n_pool, page_len = cache_nsa_kv.shape[1:3]
    past = page_table.shape[1] * page_len
    d, g, rep = A_HEAD_DIM, A_KV_GROUPS, A_REP
    hq = A_HEADS * d
    hkv = 6 * g * d
    mp = bsz * t_len
    ms = dbsz * t_new
    wdt = _MXU_DTYPE
    slopes_a = jnp.asarray(_alibi_slopes(A_HEADS))

    hs2 = [x_prompt.reshape(mp, dm), x_sample.reshape(ms, dm)]
    both = lambda fn: [fn(hh) for hh in hs2]
    rows_p, rows_s, win_p, win_s = [], [], [], []
    kv_sh = None
    for layer in range(depth):
        if layer < n_a:
            a = layer
            w_in = a_w_in[a]
            wq = w_in[:, :hq].astype(wdt)
            qt_p, qt_s = both(lambda hh: _rms_mm(hh, norm_attn[layer], wq, epi="headnorm", transpose_out=True,
                                                 gain=jnp.tile(a_q_gain[a], A_HEADS), flag=jnp.ones((hq,), _F32),
                                                 out_dtype=wdt))
            one, zero = jnp.ones((g * d,), _F32), jnp.zeros((g * d,), _F32)
            kgain = a_k_gain[a]
            kv_gain = jnp.concatenate([one, one, jnp.tile(kgain[1], g), one, jnp.tile(kgain[2], g), one])
            kv_flag = jnp.concatenate([zero, zero, one, zero, one, zero])
            wkv = w_in[:, hq:hq + hkv].astype(wdt)
            kv_proj = functools.partial(_rms_mm, gamma=norm_attn[layer], w=wkv, epi="headnorm", gain=kv_gain, flag=kv_flag,
                                        split=(4 * g * d, 2 * g * d))
            rw_p, wn_p, rw_il = kv_proj(hs2[0], interleave=4 * g)
            rw_s, wn_s = kv_proj(hs2[1])
            n_gate = 3 * A_HEADS
            w_gate = jnp.pad(w_in[:, hq + hkv:], ((0, 0), (0, LANES - n_gate))).astype(wdt)
            gates_p, gates_s = both(lambda hh: _rms_mm(hh, norm_attn[layer], w_gate, epi="sigmoid")[:, :n_gate])

            w1cat = (a_cmp_w1[a].reshape(2, CMP_RATIO, CMP_STRIDE, d, d).transpose(0, 2, 3, 1, 4)
                     .reshape(2, CMP_STRIDE * d, CMP_RATIO * d).astype(wdt))
            pe2 = jnp.pad(a_cmp_pe[a].reshape(2, CMP_RATIO, CMP_STRIDE * d),
                          ((0, 0), (0, SUBLANES - CMP_RATIO), (0, 0))).astype(wdt)
            w2 = a_cmp_w2[a].astype(wdt)
            kgain0 = kgain[0].reshape(1, d)

            kc, vct = _cmp_prompt(rw_p, w1cat, pe2, w2, kgain0, bsz=bsz, t_len=t_len)
            n_chunk = t_len // CMP_STRIDE
            n_sel_p = -(-t_len // SEL_BLOCK)
            n_blk_p = -(-n_sel_p // SUBLANES) * SUBLANES
            ovt_p = jnp.asarray(_overlap(n_chunk - CMP_RATIO + 1, n_sel_p, n_chunk, n_blk_p).T)
            chunk = 512
            et = _expand_mat_t(n_blk_p, t_len, chunk)
            gates_t = (gates_p.reshape(bsz, t_len, 3, g, rep).transpose(0, 3, 2, 4, 1)
                       .reshape(bsz, g, 3 * rep, t_len))
            o_p = _nsa_prompt(qt_p, rw_p, wn_p, gates_t, kc, vct, ovt_p, et, slopes_a,
                              bsz=bsz, t_len=t_len, chunk=chunk)

            q_s = (qt_s.astype(_F32).reshape(g, rep, d, dbsz, t_new).transpose(3, 0, 4, 1, 2)
                   .reshape(dbsz, g, t_new * rep, d))
            gt_s = (gates_s.reshape(dbsz, t_new, 3, g, rep).transpose(0, 3, 1, 4, 2)
                    .reshape(dbsz, g, t_new * rep, 3))
            kv_s = jnp.concatenate([rw_s, wn_s], axis=1).reshape(dbsz, t_new, 6, g, d)
            new_rows = lambda slot: jnp.pad(kv_s[:, :, slot].transpose(0, 2, 1, 3),
                                            ((0, 0), (0, 0), (0, SUBLANES - t_new), (0, 0)))
            cache_rows = cache_nsa_kv[a].reshape(n_pool, page_len * 4 * g, d)
            cwin = cache_nsa_win[a]
            wbuf = cwin.shape[1]
            n_chunk_s = past // CMP_STRIDE
            n_sel_s = -(-(past + t_new) // SEL_BLOCK)
            n_blk_l = -(-n_sel_s // LANES) * LANES
            ov_s = jnp.asarray(_overlap(n_chunk_s - CMP_RATIO + 1, n_sel_s, n_chunk_s, n_blk_l))
            o_s = _nsa_sample(page_table, slopes_a, cache_rows, q_s, gt_s, new_rows(2), new_rows(3), new_rows(4),
                              new_rows(5), cwin.reshape(dbsz, wbuf * 2 * g, d), w1cat, pe2, w2, kgain0, ov_s,
                              past=past, t_new=t_new, page_len=page_len)
            o_s = o_s.reshape(dbsz, g, t_new, rep, d).transpose(0, 2, 1, 3, 4).reshape(ms, hq).astype(wdt)
            w_out = a_w_out[a]

            rows_p.append(rw_il.reshape(bsz, t_len, 4, g, d))
            rows_s.append(rw_s.reshape(dbsz, t_new, 4, g, d))
            win_p.append(wn_p.reshape(bsz, t_len, 2, g, d)[:, -min(A_WINDOW, t_len):])
            win_s.append(jnp.concatenate([cwin, wn_s.reshape(dbsz, t_new, 2, g, d)], axis=1)[:, -wbuf:])
        else:
            bl = layer - n_a
            hb = B_HEADS * B_HEAD_DIM
            nk = B_KV_HEADS * B_HEAD_DIM
            wq = b_w_q[bl].astype(wdt)
            qt_p, qt_s = both(lambda hh: _rms_mm(hh, norm_attn[layer], wq, epi="headnorm", hd=B_HEAD_DIM,
                                                 transpose_out=True, gain=jnp.tile(b_q_gain[bl], B_HEADS),
                                                 flag=jnp.ones((hb,), _F32), out_dtype=wdt))
            o_p = _swa_prompt(qt_p, kv_sh[0], b_sinks[bl], bsz=bsz, t_len=t_len)

            q5 = (qt_s.astype(_F32).reshape(B_KV_HEADS, B_REP, B_HEAD_DIM, dbsz, t_new)
                  .transpose(3, 0, 2, 1, 4).reshape(dbsz, B_KV_HEADS, B_HEAD_DIM, B_REP * t_new))
            qbd = (q5[:, :, :, None, :] * jnp.eye(B_KV_HEADS, dtype=_F32)[None, :, None, :, None]
                   ).reshape(dbsz, nk, B_HEADS * t_new).astype(wdt)
            slopes_b = jnp.asarray(_alibi_slopes(B_HEADS))
            lane_consts = jnp.stack([jnp.repeat(slopes_b * LOG2E, t_new), jnp.repeat(b_sinks[bl] * LOG2E, t_new),
                                     jnp.tile(jnp.arange(t_new, dtype=_F32), B_HEADS)]
                                    + [jnp.zeros((B_HEADS * t_new,), _F32)] * (SUBLANES - 3))
            nkv = jnp.pad(kv_sh[1].reshape(dbsz, t_new, 2 * nk), ((0, 0), (0, SUBLANES - t_new), (0, 0)))
            ckv = cache_shared_win.reshape(dbsz, cache_shared_win.shape[1], 2 * nk)
            o_s = _swa_sample(qbd, ckv, nkv, lane_consts, past=past)
            o_s = (o_s.reshape(dbsz, B_HEAD_DIM, B_HEADS, t_new).transpose(0, 3, 2, 1).reshape(ms, hb).astype(wdt))
            w_out = b_w_out[bl]

        w_o = w_out.astype(wdt)
        h_s, w_1, w_2 = _mlp_cast(_mm_res(o_s, w_o, hs2[1]), norm_mlp[layer], mlp_w1[layer], mlp_w2[layer])
        hs2 = [_mlp(_mm_res(o_p, w_o, hs2[0]), norm_mlp[layer], w_1, w_2), h_s]
        if layer == n_a - 1:
            nk = B_KV_HEADS * B_HEAD_DIM
            wkv_sh = kv_w.astype(wdt)
            kv_sh = both(lambda hh: _rms_mm(
                hh, kv_norm, wkv_sh, epi="headnorm", hd=B_HEAD_DIM,
                gain=jnp.concatenate([jnp.tile(kv_k_gain, B_KV_HEADS), jnp.ones((nk,), _F32)]),
                flag=jnp.concatenate([jnp.ones((nk,), _F32), jnp.zeros((nk,), _F32)])))

    wb = cache_shared_win.shape[1]
    kv_p = kv_sh[0].reshape(bsz, t_len, 2, B_KV_HEADS, B_HEAD_DIM)
    kv_s_new = kv_sh[1].reshape(dbsz, t_new, 2, B_KV_HEADS, B_HEAD_DIM)
    return (hs2[0].reshape(bsz, t_len, dm), hs2[1].reshape(dbsz, t_new, dm),
            jnp.stack(rows_p), jnp.stack(rows_s), jnp.stack(win_p), jnp.stack(win_s),
            kv_p[:, -min(B_WINDOW, t_len):], jnp.concatenate([cache_shared_win, kv_s_new], axis=1)[:, -wb:])
```

```python
import functools

import jax
import jax.numpy as jnp
import numpy as np
from jax import lax
from jax.experimental import pallas as pl
from jax.experimental.pallas import tpu as pltpu

A_HEADS = 16
A_HEAD_DIM = 128
A_KV_GROUPS = 2
A_REP = A_HEADS // A_KV_GROUPS
CMP_LEN = 32
CMP_STRIDE = 16
CMP_RATIO = CMP_LEN // CMP_STRIDE
SEL_BLOCK = 64
SEL_TOPK = 16
N_LOCAL_FORCED = 2
A_WINDOW = 512
B_HEADS = 32
B_HEAD_DIM = 64
B_KV_HEADS = 4
B_REP = B_HEADS // B_KV_HEADS
B_WINDOW = 128
NORM_EPS = 1e-6
NEG_INF = -1e30
FORCE_BONUS = 1e4

LANES = 128
SUBLANES = 8
X2_PITCH = SUBLANES + 1

LOG2E = float(np.log2(np.e))
MASKED_DIST = 1e33

_MXU_DTYPE = jnp.bfloat16
_F32 = jnp.float32


def _cparams(sem, vmem_mb):
    return pltpu.CompilerParams(dimension_semantics=sem, vmem_limit_bytes=vmem_mb * 1024 * 1024)


def _dot(a, b):
    return jnp.dot(a, b, preferred_element_type=_F32)


def _dot_nt(a, b):
    return lax.dot_general(a, b, (((1,), (1,)), ((), ())), preferred_element_type=_F32)


def _split3(a):
    hi = a.astype(_MXU_DTYPE)
    r1 = a - hi.astype(_F32)
    mid = r1.astype(_MXU_DTYPE)
    lo = (r1 - mid.astype(_F32)).astype(_MXU_DTYPE)
    return hi, mid, lo


def _dot_exact(a, b01):
    bm = b01.astype(_MXU_DTYPE)
    hi, mid, lo = _split3(a)
    return _dot(hi, bm) + _dot(mid, bm) + _dot(lo, bm)


def _dot_exact_rhs(a01, b):
    am = a01.astype(_MXU_DTYPE)
    hi, mid, lo = _split3(b)
    return _dot(am, hi) + _dot(am, mid) + _dot(am, lo)


def _idiv(x, n):
    if n & (n - 1) == 0:
        return lax.shift_right_logical(x, jnp.int32(n.bit_length() - 1))
    return x // n


def _pick_tile(m, want):
    if m <= want:
        return m
    for t in range(want - want % SUBLANES, 0, -SUBLANES):
        if m % t == 0:
            return t
    raise ValueError(f"no tile for {m}")


def _alibi_slopes(n_heads):
    return np.asarray(2.0 ** (-8.0 * np.arange(1, n_heads + 1) / n_heads), dtype=np.float32)


def _rms_rows(x, gain):
    ms = jnp.mean(x * x, axis=-1, keepdims=True)
    return x * lax.rsqrt(ms + NORM_EPS) * gain


def _head_norm(y, gain, flag, hd):
    outs = []
    for c in range(y.shape[1] // LANES):
        yc = y[:, c * LANES:(c + 1) * LANES]
        y2 = yc * yc
        if hd == LANES:
            ms = jnp.mean(y2, axis=-1, keepdims=True)
        else:
            row = _idiv(lax.broadcasted_iota(jnp.int32, (LANES, LANES), 0), hd)
            col = _idiv(lax.broadcasted_iota(jnp.int32, (LANES, LANES), 1), hd)
            ms = _dot_exact(y2, (row == col).astype(_F32)) * (1.0 / hd)
        yn = yc * lax.rsqrt(ms + NORM_EPS) * gain[:, c * LANES:(c + 1) * LANES]
        outs.append(jnp.where(flag[:, c * LANES:(c + 1) * LANES] > 0, yn, yc))
    return outs[0] if len(outs) == 1 else jnp.concatenate(outs, axis=1)


def _rms_mm_body(x_ref, gam_ref, w_ref, gain_ref, flag_ref, *refs, epi, hd, transpose_out, interleave):
    o_refs, xn_ref = refs[:-1], refs[-1]
    if interleave:
        o_refs, il_ref = o_refs[:-1], o_refs[-1]

    @pl.when(pl.program_id(1) == 0)
    def _():
        xn_ref[...] = _rms_rows(x_ref[...], gam_ref[...]).astype(xn_ref.dtype)

    y = _dot(xn_ref[...], w_ref[...])
    if epi == "headnorm":
        y = _head_norm(y, gain_ref[...], flag_ref[...], hd)
    elif epi == "sigmoid":
        y = jax.nn.sigmoid(y)
    if transpose_out:
        y = y.T
    col = 0
    for o_ref in o_refs:
        o_ref[...] = y[:, col:col + o_ref.shape[1]].astype(o_ref.dtype)
        col += o_ref.shape[1]
    if interleave:
        for c in range(interleave):
            il_ref[pl.ds(c, y.shape[0], stride=interleave), :] = y[:, c * LANES:(c + 1) * LANES]


def _rms_mm(x, gamma, w, *, epi="none", gain=None, flag=None, hd=LANES, out_dtype=_F32, transpose_out=False,
            split=None, interleave=0, tm=512, tn=2048):
    m, k = x.shape
    n = w.shape[1]
    tm = _pick_tile(m, tm)
    tn = _pick_tile(n, tn)
    if gain is None:
        gain = jnp.ones((n,), _F32)
        flag = jnp.zeros((n,), _F32)
    if transpose_out:
        out_spec = pl.BlockSpec((tn, tm), lambda i, j: (j, i))
        out_shape = jax.ShapeDtypeStruct((n, m), out_dtype)
    elif split is not None:
        assert tn == n and sum(split) == n
        out_spec = [pl.BlockSpec((tm, ns), lambda i, j: (i, 0)) for ns in split]
        out_shape = [jax.ShapeDtypeStruct((m, ns), out_dtype) for ns in split]
        if interleave:
            out_spec.append(pl.BlockSpec((tm * interleave, LANES), lambda i, j: (i, 0)))
            out_shape.append(jax.ShapeDtypeStruct((m * interleave, LANES), out_dtype))
    else:
        out_spec = pl.BlockSpec((tm, tn), lambda i, j: (i, j))
        out_shape = jax.ShapeDtypeStruct((m, n), out_dtype)
    return pl.pallas_call(
        functools.partial(_rms_mm_body, epi=epi, hd=hd, transpose_out=transpose_out, interleave=interleave),
        grid=(m // tm, n // tn),
        in_specs=[
            pl.BlockSpec((tm, k), lambda i, j: (i, 0)),
            pl.BlockSpec((1, k), lambda i, j: (0, 0)),
            pl.BlockSpec((k, tn), lambda i, j: (0, j)),
            pl.BlockSpec((1, tn), lambda i, j: (0, j)),
            pl.BlockSpec((1, tn), lambda i, j: (0, j)),
        ],
        out_specs=out_spec,
        out_shape=out_shape,
        scratch_shapes=[pltpu.VMEM((tm, k), _MXU_DTYPE)],
        compiler_params=_cparams(("arbitrary", "arbitrary"), 56),
    )(x, gamma.reshape(1, k), w, gain.reshape(1, n), flag.reshape(1, n))


def _mm_res_body(a_ref, w_ref, h_ref, o_ref):
    o_ref[...] = h_ref[...] + _dot(a_ref[...], w_ref[...])


def _mm_res(a, w, h, *, tm=512, tn=2048):
    m, k = a.shape
    n = w.shape[1]
    tm = _pick_tile(m, tm)
    tn = _pick_tile(n, tn)
    return pl.pallas_call(
        _mm_res_body,
        grid=(m // tm, n // tn),
        in_specs=[
            pl.BlockSpec((tm, k), lambda i, j: (i, 0)),
            pl.BlockSpec((k, tn), lambda i, j: (0, j)),
            pl.BlockSpec((tm, tn), lambda i, j: (i, j)),
        ],
        out_specs=pl.BlockSpec((tm, tn), lambda i, j: (i, j)),
        out_shape=jax.ShapeDtypeStruct((m, n), _F32),
        compiler_params=_cparams(("arbitrary", "arbitrary"), 56),
    )(a, w, h)


def _mlp_body(h_ref, gam_ref, w1_ref, w2_ref, o_ref, xn_ref):
    @pl.when(pl.program_id(1) == 0)
    def _():
        h = h_ref[...]
        xn_ref[...] = _rms_rows(h, gam_ref[...]).astype(xn_ref.dtype)
        o_ref[...] = h

    a = jnp.square(jnp.maximum(_dot(xn_ref[...], w1_ref[...]), 0.0))
    o_ref[...] += _dot(a.astype(_MXU_DTYPE), w2_ref[...])


def _mlp_cast_body(h_ref, gam_ref, w1_ref, w2_ref, o_ref, w1b_ref, w2b_ref, xn_ref):
    @pl.when(pl.program_id(0) == 0)
    def _():
        h = h_ref[...]
        xn_ref[...] = _rms_rows(h, gam_ref[...]).astype(xn_ref.dtype)
        o_ref[...] = h

    w1b = w1_ref[...].astype(w1b_ref.dtype)
    w2b = w2_ref[...].astype(w2b_ref.dtype)
    w1b_ref[...] = w1b
    w2b_ref[...] = w2b
    a = jnp.square(jnp.maximum(_dot(xn_ref[...], w1b), 0.0))
    o_ref[...] += _dot(a.astype(_MXU_DTYPE), w2b)


def _mlp_cast(h, gamma, w1, w2, *, tf=512):
    m, d = h.shape
    f = w1.shape[1]
    tf = _pick_tile(f, tf)
    return pl.pallas_call(
        _mlp_cast_body,
        grid=(f // tf,),
        in_specs=[
            pl.BlockSpec((m, d), lambda j: (0, 0)),
            pl.BlockSpec((1, d), lambda j: (0, 0)),
            pl.BlockSpec((d, tf), lambda j: (0, j)),
            pl.BlockSpec((tf, d), lambda j: (j, 0)),
        ],
        out_specs=[pl.BlockSpec((m, d), lambda j: (0, 0)),
                   pl.BlockSpec((d, tf), lambda j: (0, j)),
                   pl.BlockSpec((tf, d), lambda j: (j, 0))],
        out_shape=[jax.ShapeDtypeStruct((m, d), _F32), jax.ShapeDtypeStruct(w1.shape, _MXU_DTYPE),
                   jax.ShapeDtypeStruct(w2.shape, _MXU_DTYPE)],
        scratch_shapes=[pltpu.VMEM((m, d), _MXU_DTYPE)],
        compiler_params=_cparams(("arbitrary",), 56),
    )(h, gamma.reshape(1, d), w1, w2)


def _mlp(h, gamma, w1, w2, *, tm=1024, tf=512):
    m, d = h.shape
    f = w1.shape[1]
    tm = _pick_tile(m, tm)
    tf = _pick_tile(f, tf)
    return pl.pallas_call(
        _mlp_body,
        grid=(m // tm, f // tf),
        in_specs=[
            pl.BlockSpec((tm, d), lambda i, j: (i, 0)),
            pl.BlockSpec((1, d), lambda i, j: (0, 0)),
            pl.BlockSpec((d, tf), lambda i, j: (0, j)),
            pl.BlockSpec((tf, d), lambda i, j: (j, 0)),
        ],
        out_specs=pl.BlockSpec((tm, d), lambda i, j: (i, 0)),
        out_shape=jax.ShapeDtypeStruct((m, d), _F32),
        scratch_shapes=[pltpu.VMEM((tm, d), _MXU_DTYPE)],
        compiler_params=_cparams(("arbitrary", "arbitrary"), 56),
    )(h, gamma.reshape(1, d), w1, w2)


def _block_scores(imp, blk, cur, n_sel):
    score, _ = _block_scores_forced(imp, blk, cur, n_sel)
    return score


def _block_scores_forced(imp, blk, cur, n_sel):
    valid = blk <= cur
    forced = (blk == 0) | (((cur - blk) < N_LOCAL_FORCED) & valid)
    score = jnp.where(valid, imp + FORCE_BONUS * forced.astype(_F32), NEG_INF)
    return jnp.where(blk < n_sel, score, -jnp.inf), forced


def _select_blocks_rows(imp, qpos, n_sel, between=()):
    assert FORCE_BONUS > 2 * A_REP
    n_forced = 1 + N_LOCAL_FORCED
    blk = lax.broadcasted_iota(jnp.int32, imp.shape, 1)
    s, forced = _block_scores_forced(imp, blk, _idiv(qpos, SEL_BLOCK), n_sel)
    sel = jnp.where(forced, 1.0, 0.0)
    s = jnp.where(forced, -jnp.inf, s)
    pending = list(between)
    for _ in range(min(SEL_TOPK, n_sel) - n_forced):
        hit = blk == jnp.argmax(s, axis=-1, keepdims=True).astype(jnp.int32)
        sel = jnp.where(hit, 1.0, sel)
        s = jnp.where(hit, -jnp.inf, s)
        if pending:
            pending.pop(0)()
    for work in pending:
        work()
    return sel


def _select_blocks_cols(imp_t, qpos, n_sel):
    blk = lax.broadcasted_iota(jnp.int32, imp_t.shape, 0)
    s = _block_scores(imp_t, blk, _idiv(qpos, SEL_BLOCK), n_sel)
    beaten = jnp.zeros(imp_t.shape, _F32)
    for j in range(n_sel):
        sj = s[j:j + 1, :]
        tie = jnp.where(blk > j, 1.0, 0.0)
        beaten = beaten + jnp.where(sj > s, 1.0, 0.0) + jnp.where(sj == s, tie, 0.0)
    return jnp.where((beaten < min(SEL_TOPK, n_sel)) & (blk < n_sel), 1.0, 0.0)


def _compress_x2(x2, w1cat, pe2, w2, n_chunk):
    return _compress_finish(_dot(x2.astype(_MXU_DTYPE), w1cat), w1cat, pe2, w2, n_chunk)


def _compress_finish(h, w1cat, pe2, w2, n_chunk):
    d = A_HEAD_DIM
    c = _dot(pe2, w1cat)
    part0 = h[:, :d] + c[0:1, :d]
    part1 = h[:, d:] + c[1:2, d:]
    pieces = []
    for s in range(h.shape[0] // n_chunk):
        p1 = part1[s * n_chunk:(s + 1) * n_chunk]
        pieces.append(part0[s * n_chunk:(s + 1) * n_chunk] + pltpu.roll(p1, n_chunk - 1, 0))
    hh = pieces[0] if len(pieces) == 1 else jnp.concatenate(pieces, axis=0)
    return _dot(jax.nn.gelu(hh).astype(_MXU_DTYPE), w2)


def _softmax_cols(s_t, distm, slope2, scale2, sink2=None):
    sc = s_t * scale2 - slope2 * distm
    m = jnp.max(sc, axis=0, keepdims=True)
    if sink2 is not None:
        m = jnp.maximum(m, sink2)
    e = jnp.exp2(sc - m)
    l = jnp.sum(e, axis=0, keepdims=True)
    if sink2 is not None:
        l = l + jnp.exp2(sink2 - m)
    return e, m, l


def _cmp_prompt_body(kraw_ref, vraw_ref, w1_ref, pe_ref, w2_ref, kg_ref, kc_ref, vct_ref, *, n_chunk):
    for slot, src in enumerate((kraw_ref, vraw_ref)):
        x2 = jnp.concatenate([src[pl.ds(s, n_chunk, stride=CMP_STRIDE), :] for s in range(CMP_STRIDE)], axis=1)
        y = _compress_x2(x2, w1_ref[slot], pe_ref[slot], w2_ref[slot], n_chunk)
        if slot == 0:
            kc_ref[...] = _rms_rows(y, kg_ref[...])
        else:
            vct_ref[...] = y.T


def _cmp_prompt(kv, w1cat, pe2, w2, kgain0, *, bsz, t_len):
    n_chunk = t_len // CMP_STRIDE
    d = A_HEAD_DIM
    g = A_KV_GROUPS
    return pl.pallas_call(
        functools.partial(_cmp_prompt_body, n_chunk=n_chunk),
        grid=(bsz, g),
        in_specs=[
            pl.BlockSpec((t_len, d), lambda b, gi: (b, gi)),
            pl.BlockSpec((t_len, d), lambda b, gi: (b, g + gi)),
            pl.BlockSpec(w1cat.shape, lambda b, gi: (0, 0, 0)),
            pl.BlockSpec(pe2.shape, lambda b, gi: (0, 0, 0)),
            pl.BlockSpec(w2.shape, lambda b, gi: (0, 0, 0)),
            pl.BlockSpec((1, d), lambda b, gi: (0, 0)),
        ],
        out_specs=[pl.BlockSpec((None, None, n_chunk, d), lambda b, gi: (b, gi, 0, 0)),
                   pl.BlockSpec((None, None, d, n_chunk), lambda b, gi: (b, gi, 0, 0))],
        out_shape=[jax.ShapeDtypeStruct((bsz, g, n_chunk, d), _F32), jax.ShapeDtypeStruct((bsz, g, d, n_chunk), _F32)],
        compiler_params=_cparams(("arbitrary", "arbitrary"), 32),
    )(kv, kv, w1cat, pe2, w2, kgain0)


def _nsa_prompt_body(slope_ref, qt_ref, gt_ref, kc_ref, vct_ref, ks_ref, vs_ref, kw_ref, vw_ref, ovt_ref, et_ref,
                     o_ref, ksb, vst, kwb, vwt, *, tq, t_len, n_cmp, n_sel, chunk):
    gi = pl.program_id(1)
    i = pl.program_id(2)
    d = A_HEAD_DIM
    rep = A_REP
    scale2 = d ** -0.5 * LOG2E
    n_chunks = t_len // chunk
    n_wblk = A_WINDOW // tq + 1

    @pl.when(i == 0)
    def _():
        ksb[...] = ks_ref[...].astype(ksb.dtype)
        kwb[...] = kw_ref[...].astype(kwb.dtype)
        for c in range(n_chunks):
            vst[c] = vs_ref[c * chunk:(c + 1) * chunk, :].T.astype(vst.dtype)
        for c in range(t_len // tq):
            vwt[c] = vw_ref[c * tq:(c + 1) * tq, :].T.astype(vwt.dtype)

    t0 = i * tq
    qt = qt_ref[...]
    qst = jnp.concatenate([qt[r * d:(r + 1) * d, :] for r in range(rep)], axis=1)
    tpos = t0 + lax.broadcasted_iota(jnp.int32, (1, tq), 1)
    slope2 = [slope_ref[gi * rep + r] * LOG2E for r in range(rep)]
    head = lambda a, r: a[:, r * tq:(r + 1) * tq]

    def masked_dist(valid, dist):
        return jnp.where(valid, dist.astype(_F32), MASKED_DIST)

    n_c = kc_ref.shape[0]
    cidx = lax.broadcasted_iota(jnp.int32, (n_c, 1), 0)
    cend = cidx * CMP_STRIDE + (CMP_LEN - 1)
    distm = masked_dist((cend <= tpos) & (cidx < n_cmp), tpos - cend)
    s_t = _dot(kc_ref[...].astype(_MXU_DTYPE), qst)
    any_valid = (tpos >= CMP_LEN - 1).astype(_F32)
    p_heads = []
    for r in range(rep):
        e, _, l = _softmax_cols(head(s_t, r), distm, slope2[r], scale2)
        p_heads.append(e * (any_valid / l))
    p_sum = p_heads[0]
    for r in range(1, rep):
        p_sum = p_sum + p_heads[r]
    o_cmp = _dot(vct_ref[...].astype(_MXU_DTYPE), jnp.concatenate(p_heads, axis=1).astype(_MXU_DTYPE))
    imp_t = _dot_exact_rhs(ovt_ref[...], p_sum)
    sel_t = _select_blocks_cols(imp_t, tpos, n_sel).astype(_MXU_DTYPE)

    n_live = (t0 + tq + chunk - 1) // chunk

    def sel_step(c, carry):
        m_i, l_i, acc = carry
        base = pl.multiple_of(c * chunk, chunk)
        kpos = base + lax.broadcasted_iota(jnp.int32, (chunk, 1), 0)
        picked = _dot(et_ref[c], sel_t)
        distm = masked_dist((picked > 0.5) & (kpos <= tpos), tpos - kpos)
        s_t = _dot(ksb[pl.ds(base, chunk), :], qst)
        es, ms, ls = [], [], []
        for r in range(rep):
            sc = head(s_t, r) * scale2 - slope2[r] * distm
            m_new = jnp.maximum(head(m_i, r), jnp.max(sc, axis=0, keepdims=True))
            e = jnp.exp2(sc - m_new)
            es.append(e.astype(_MXU_DTYPE))
            ms.append(m_new)
            ls.append(jnp.sum(e, axis=0, keepdims=True))
        m_new = jnp.concatenate(ms, axis=1)
        alpha = jnp.exp2(m_i - m_new)
        l_new = alpha * l_i + jnp.concatenate(ls, axis=1)
        acc = alpha * acc + _dot(vst[c], jnp.concatenate(es, axis=1))
        return m_new, l_new, acc

    init = (jnp.full((1, rep * tq), NEG_INF, _F32), jnp.zeros((1, rep * tq), _F32), jnp.zeros((d, rep * tq), _F32))
    _, l_s, acc_s = lax.fori_loop(0, n_live, sel_step, init)
    o_sel = acc_s * (1.0 / l_s)

    wlen = n_wblk * tq
    wb0 = jnp.maximum(i - (n_wblk - 1), 0)
    start = pl.multiple_of(wb0 * tq, tq)
    kpos = start + lax.broadcasted_iota(jnp.int32, (wlen, 1), 0)
    dw = tpos - kpos
    distm = masked_dist((dw >= 0) & (dw < A_WINDOW), dw)
    s_t = _dot(kwb[pl.ds(start, wlen), :], qst)
    es, ls = [], []
    for r in range(rep):
        e, _, l = _softmax_cols(head(s_t, r), distm, slope2[r], scale2)
        es.append(e.astype(_MXU_DTYPE))
        ls.append(l)
    vw = jnp.concatenate([vwt[wb0 + j] for j in range(n_wblk)], axis=1)
    o_win = _dot(vw, jnp.concatenate(es, axis=1)) * (1.0 / jnp.concatenate(ls, axis=1))

    gt = gt_ref[...]
    for r in range(rep):
        o_r = (gt[r:r + 1] * head(o_cmp, r) + gt[rep + r:rep + r + 1] * head(o_sel, r)
               + gt[2 * rep + r:2 * rep + r + 1] * head(o_win, r))
        o_ref[:, r * d:(r + 1) * d] = o_r.T.astype(o_ref.dtype)


def _nsa_prompt(qt_all, rows, win, gates_t, kc, vct, ovt, et, slopes, *, bsz, t_len, tq=256, chunk=512):
    d = A_HEAD_DIM
    g = A_KV_GROUPS
    rep = A_REP
    nq = t_len // tq
    n_chunk = kc.shape[2]
    n_cmp = n_chunk - CMP_RATIO + 1
    n_sel = -(-t_len // SEL_BLOCK)
    assert t_len % tq == 0 and A_WINDOW % tq == 0 and t_len >= A_WINDOW + tq
    assert t_len % chunk == 0 and chunk % SEL_BLOCK == 0
    kvspec = lambda col: pl.BlockSpec((t_len, d), lambda b, gi, i, col=col: (b, col + gi))
    return pl.pallas_call(
        functools.partial(_nsa_prompt_body, tq=tq, t_len=t_len, n_cmp=n_cmp, n_sel=n_sel, chunk=chunk),
        grid=(bsz, g, nq),
        in_specs=[
            pl.BlockSpec(memory_space=pltpu.SMEM),
            pl.BlockSpec((rep * d, tq), lambda b, gi, i: (gi, b * nq + i)),
            pl.BlockSpec((None, None, 3 * rep, tq), lambda b, gi, i: (b, gi, 0, i)),
            pl.BlockSpec((None, None, n_chunk, d), lambda b, gi, i: (b, gi, 0, 0)),
            pl.BlockSpec((None, None, d, n_chunk), lambda b, gi, i: (b, gi, 0, 0)),
            kvspec(2 * g), kvspec(3 * g), kvspec(0), kvspec(g),
            pl.BlockSpec(ovt.shape, lambda b, gi, i: (0, 0)),
            pl.BlockSpec(et.shape, lambda b, gi, i: (0, 0, 0)),
        ],
        out_specs=pl.BlockSpec((tq, rep * d), lambda b, gi, i: (b * nq + i, gi)),
        out_shape=jax.ShapeDtypeStruct((bsz * t_len, g * rep * d), _MXU_DTYPE),
        scratch_shapes=[pltpu.VMEM((t_len, d), _MXU_DTYPE), pltpu.VMEM((t_len // chunk, d, chunk), _MXU_DTYPE),
                        pltpu.VMEM((t_len, d), _MXU_DTYPE), pltpu.VMEM((t_len // tq, d, tq), _MXU_DTYPE)],
        compiler_params=_cparams(("arbitrary", "arbitrary", "arbitrary"), 48),
    )(slopes, qt_all, gates_t, kc, vct, rows, rows, win, win, ovt, et)


def _nsa_sample_body(pt_ref, slope_ref, *refs, pps, past, t_new, n_cmp, n_sel):
    pages = refs[:pps]
    (q_ref, gt_ref, ksn_ref, vsn_ref, kwn_ref, vwn_ref, cwin_ref, w1_ref, pe_ref, w2_ref, kg_ref, ov_ref,
     o_ref, x2, hcmp, ksb, vsb, emat) = refs[pps:]
    b = pl.program_id(0)
    pp = pl.program_id(1)
    n_pp = pl.num_programs(1)
    d = A_HEAD_DIM
    g = A_KV_GROUPS
    rep = A_REP
    rows = t_new * rep
    scale = d ** -0.5
    sg_stride = 4 * g
    page_len = pages[0].shape[0] // sg_stride
    cpp = page_len // CMP_STRIDE
    assert cpp == SUBLANES
    n_chunk = past // CMP_STRIDE
    n_rb = n_chunk // cpp

    @pl.when((b == 0) & (pp == 0))
    def _():
        blk = lax.broadcasted_iota(jnp.int32, emat.shape, 0)
        key = _idiv(lax.broadcasted_iota(jnp.int32, emat.shape, 1), SEL_BLOCK)
        emat[...] = (blk == key).astype(emat.dtype)
        x2[...] = jnp.zeros(x2.shape, x2.dtype)

    step_chunks = pps * cpp
    par = pp % 2

    def step_matmul(buf, step):
        for slot in range(2):
            slabs = [jnp.concatenate([x2[buf, slot, rb, pl.ds(s * X2_PITCH, cpp), :] for s in range(CMP_STRIDE)], axis=1)
                     for rb in range(g * pps)]
            h_step = _dot(jnp.concatenate(slabs, axis=0).astype(_MXU_DTYPE), w1_ref[slot])
            for gi in range(g):
                hrow = pl.multiple_of(gi * n_chunk + step * step_chunks, step_chunks)
                hcmp[slot, pl.ds(hrow, step_chunks), :] = h_step[gi * step_chunks:(gi + 1) * step_chunks]

    step_matmul(1 - par, (pp + n_pp - 1) % n_pp)
    for pi in range(pps):
        pg = pages[pi]
        page_no = pp * pps + pi
        for gi in range(g):
            for slot in range(2):
                a = pg[pl.ds(slot * g + gi, page_len, stride=sg_stride), :]
                for c in range(cpp):
                    for hs in range(CMP_STRIDE // SUBLANES):
                        p0 = c * CMP_STRIDE + hs * SUBLANES
                        x2[par, slot, gi * pps + pi, pl.ds(hs * SUBLANES * X2_PITCH + c, SUBLANES, stride=X2_PITCH), :] = (
                            a[p0:p0 + SUBLANES])
            krow = pl.multiple_of(page_no * page_len, page_len)
            ksb[gi, pl.ds(krow, page_len), :] = pg[pl.ds(2 * g + gi, page_len, stride=sg_stride), :].astype(ksb.dtype)
            vsb[gi, pl.ds(krow, page_len), :] = pg[pl.ds(3 * g + gi, page_len, stride=sg_stride), :].astype(vsb.dtype)

    @pl.when(pp == n_pp - 1)
    def _():
        step_matmul(par, pp)
        kc_all = _rms_rows(_compress_finish(hcmp[0], w1_ref[0], pe_ref[0], w2_ref[0], n_chunk), kg_ref[...])
        vc_all = _compress_finish(hcmp[1], w1_ref[1], pe_ref[1], w2_ref[1], n_chunk)
        rowi = lax.broadcasted_iota(jnp.int32, (rows, 1), 0)
        qpos = past + _idiv(rowi, rep)
        head_r = rowi - _idiv(rowi, rep) * rep
        pad_new = jnp.zeros((LANES - ksn_ref.shape[1], d), _F32)

        def padded(ref, gi):
            return jnp.concatenate([ref[gi], pad_new], axis=0).astype(_MXU_DTYPE)

        def biased(s, valid, dist, slope):
            return jnp.where(valid, s * scale - slope * dist, NEG_INF)

        def two_part(s_p, s_n, v_p, v_n):
            m = jnp.maximum(jnp.max(s_p, axis=-1, keepdims=True), jnp.max(s_n, axis=-1, keepdims=True))
            e_p = jnp.exp(s_p - m)
            e_n = jnp.exp(s_n - m)
            l = jnp.sum(e_p, axis=-1, keepdims=True) + jnp.sum(e_n, axis=-1, keepdims=True)
            return (_dot(e_p.astype(_MXU_DTYPE), v_p) + _dot(e_n.astype(_MXU_DTYPE), v_n)) / l

        qs, slopes, p_cs, imps = [], [], [], []
        cidx = lax.broadcasted_iota(jnp.int32, (1, n_chunk), 1)
        cend = cidx * CMP_STRIDE + (CMP_LEN - 1)
        for gi in range(g):
            q = q_ref[gi].astype(_MXU_DTYPE)
            slope = jnp.zeros((rows, 1), _F32)
            for r in range(rep):
                slope = jnp.where(head_r == r, slope_ref[gi * rep + r], slope)
            kc = kc_all[gi * n_chunk:(gi + 1) * n_chunk].astype(_MXU_DTYPE)
            s = biased(_dot_nt(q, kc), (cend <= qpos) & (cidx < n_cmp), (qpos - cend).astype(_F32), slope)
            e = jnp.exp(s - jnp.max(s, axis=-1, keepdims=True))
            p_c = e / jnp.sum(e, axis=-1, keepdims=True) * (qpos >= CMP_LEN - 1).astype(_F32)
            p3 = p_c.reshape(t_new, rep, n_chunk)
            p_sum = jnp.broadcast_to(jnp.sum(p3, axis=1, keepdims=True), p3.shape).reshape(rows, n_chunk)
            imps.append(_dot_exact(p_sum, ov_ref[...]))
            p_cs.append(p_c.astype(_MXU_DTYPE))
            qs.append(q)
            slopes.append(slope)

        npos = past + lax.broadcasted_iota(jnp.int32, (1, LANES), 1)
        wbuf = cwin_ref.shape[0] // (2 * g)
        wpos = past - wbuf + lax.broadcasted_iota(jnp.int32, (1, wbuf), 1)
        o_wins, s_raw, s_raw_new = [None] * g, [None] * g, [None] * g

        def selected_scores(gi):
            s_raw[gi] = _dot_nt(qs[gi], ksb[gi])
            s_raw_new[gi] = _dot_nt(qs[gi], padded(ksn_ref, gi))

        def window_branch(gi):
            q, slope = qs[gi], slopes[gi]
            kw = cwin_ref[pl.ds(gi, wbuf, stride=2 * g), :].astype(_MXU_DTYPE)
            vw = cwin_ref[pl.ds(g + gi, wbuf, stride=2 * g), :].astype(_MXU_DTYPE)
            dw = qpos - wpos
            s_p = biased(_dot_nt(q, kw), (dw >= 0) & (dw < A_WINDOW), dw.astype(_F32), slope)
            dn = qpos - npos
            s_n = biased(_dot_nt(q, padded(kwn_ref, gi)), (dn >= 0) & (dn < A_WINDOW), dn.astype(_F32), slope)
            o_wins[gi] = two_part(s_p, s_n, vw, padded(vwn_ref, gi))

        fillers = [functools.partial(fn, gi) for gi in range(g) for fn in (selected_scores, window_branch)]
        selm_all = _select_blocks_rows(jnp.concatenate(imps, axis=0), jnp.concatenate([qpos] * g, axis=0), n_sel,
                                       between=fillers)

        kpos = lax.broadcasted_iota(jnp.int32, (1, past), 1)
        new_blk = past // SEL_BLOCK
        for gi in range(g):
            q, slope = qs[gi], slopes[gi]
            selm = selm_all[gi * rows:(gi + 1) * rows]
            o_cmp = _dot(p_cs[gi], vc_all[gi * n_chunk:(gi + 1) * n_chunk].astype(_MXU_DTYPE))
            picked = _dot(selm.astype(_MXU_DTYPE), emat[...])
            s_p = biased(s_raw[gi], (picked > 0.5) & (kpos <= qpos), (qpos - kpos).astype(_F32), slope)
            sel_new = selm[:, new_blk:new_blk + 1] > 0.5
            s_n = biased(s_raw_new[gi], sel_new & (npos <= qpos), (qpos - npos).astype(_F32), slope)
            o_sel = two_part(s_p, s_n, vsb[gi], padded(vsn_ref, gi))

            gt = gt_ref[gi]
            o_ref[gi] = gt[:, 0:1] * o_cmp + gt[:, 1:2] * o_sel + gt[:, 2:3] * o_wins[gi]


def _nsa_sample(page_table, slopes, cache_rows, q_s, gt_s, ksn, vsn, kwn, vwn, cwin, w1cat, pe2, w2, kgain0, ov_s,
                *, past, t_new, page_len, pps=16):
    dbsz, n_pages = page_table.shape
    d = A_HEAD_DIM
    g = A_KV_GROUPS
    rows = t_new * A_REP
    n_chunk = past // CMP_STRIDE
    n_cmp = n_chunk - CMP_RATIO + 1
    n_sel = -(-(past + t_new) // SEL_BLOCK)
    n_blk_l = ov_s.shape[1]
    assert past % CMP_STRIDE == 0 and t_new < CMP_STRIDE and n_pages % pps == 0 and past % SEL_BLOCK == 0
    assert t_new <= SEL_BLOCK and n_sel <= n_blk_l
    assert past // SEL_BLOCK >= N_LOCAL_FORCED and min(SEL_TOPK, n_sel) >= 1 + N_LOCAL_FORCED
    page_rows = page_len * 4 * g
    cpp = page_len // CMP_STRIDE

    def page_spec(pi):
        return pl.BlockSpec((None, page_rows, d), lambda b, pp, pt, pi=pi: (pt[b * n_pages + pp * pps + pi], 0, 0))

    per_b = lambda shape: pl.BlockSpec((None,) + shape, lambda b, pp, pt: (b,) + (0,) * len(shape))
    whole = lambda a: pl.BlockSpec(a.shape, lambda b, pp, pt: (0,) * a.ndim)
    grid_spec = pltpu.PrefetchScalarGridSpec(
        num_scalar_prefetch=1,
        grid=(dbsz, n_pages // pps),
        in_specs=[pl.BlockSpec(memory_space=pltpu.SMEM)] + [page_spec(pi) for pi in range(pps)] + [
            per_b((g, rows, d)), per_b((g, rows, 3)),
            per_b(ksn.shape[1:]), per_b(vsn.shape[1:]), per_b(kwn.shape[1:]), per_b(vwn.shape[1:]),
            per_b(cwin.shape[1:]),
            whole(w1cat), whole(pe2), whole(w2), whole(kgain0), whole(ov_s),
        ],
        out_specs=per_b((g, rows, d)),
        scratch_shapes=[
            pltpu.VMEM((2, 2, g * pps, CMP_STRIDE * X2_PITCH, d), _F32), pltpu.VMEM((2, g * n_chunk, CMP_RATIO * d), _F32),
            pltpu.VMEM((g, past, d), _MXU_DTYPE), pltpu.VMEM((g, past, d), _MXU_DTYPE),
            pltpu.VMEM((n_blk_l, past), _MXU_DTYPE),
        ],
    )
    return pl.pallas_call(
        functools.partial(_nsa_sample_body, pps=pps, past=past, t_new=t_new, n_cmp=n_cmp, n_sel=n_sel),
        grid_spec=grid_spec,
        out_shape=jax.ShapeDtypeStruct((dbsz, g, rows, d), _F32),
        compiler_params=_cparams(("arbitrary", "arbitrary"), 60),
    )(page_table.reshape(-1), slopes, *([cache_rows] * pps), q_s, gt_s, ksn, vsn, kwn, vwn, cwin,
      w1cat, pe2, w2, kgain0, ov_s)


def _swa_prompt_body(sink_ref, qt_ref, kv_ref, o_ref, kb, vt, *, tq, t_len):
    i = pl.program_id(1)
    hd = B_HEAD_DIM
    nk = B_KV_HEADS * hd
    scale2 = hd ** -0.5 * LOG2E
    slopes = _alibi_slopes(B_HEADS)
    n_wblk = (B_WINDOW + tq) // LANES
    wlen = n_wblk * LANES

    @pl.when(i == 0)
    def _():
        kb[...] = kv_ref[:, :nk].astype(kb.dtype)
        for c in range(t_len // LANES):
            vt[c] = kv_ref[c * LANES:(c + 1) * LANES, nk:].T.astype(vt.dtype)

    t0 = i * tq
    wb0 = jnp.maximum(i * (tq // LANES) - B_WINDOW // LANES, 0)
    start = pl.multiple_of(wb0 * LANES, LANES)
    tpos = t0 + lax.broadcasted_iota(jnp.int32, (1, tq), 1)
    kpos = start + lax.broadcasted_iota(jnp.int32, (wlen, 1), 0)
    dw = tpos - kpos
    distm = jnp.where((dw >= 0) & (dw < B_WINDOW), dw.astype(_F32), MASKED_DIST)
    kwin = kb[pl.ds(start, wlen), :]
    vwin_t = jnp.concatenate([vt[wb0 + j] for j in range(n_wblk)], axis=1)
    qt = qt_ref[...]
    zeros = jnp.zeros((hd, tq), qt.dtype)
    for kh in range(B_KV_HEADS):
        pb = (kh // 2) * LANES
        blocks = []
        for r in range(B_REP):
            h = kh * B_REP + r
            qh = qt[h * hd:(h + 1) * hd, :]
            blocks.append(jnp.concatenate([qh, zeros] if kh % 2 == 0 else [zeros, qh], axis=0))
        s_t = _dot(kwin[:, pb:pb + LANES], jnp.concatenate(blocks, axis=1))
        ps, ls = [], []
        for r in range(B_REP):
            h = kh * B_REP + r
            e, _, l = _softmax_cols(s_t[:, r * tq:(r + 1) * tq], distm, float(slopes[h]) * LOG2E, scale2,
                                    sink2=sink_ref[h] * LOG2E)
            ps.append(e.astype(_MXU_DTYPE))
            ls.append(l)
        o_t = (_dot(vwin_t[kh * hd:(kh + 1) * hd, :], jnp.concatenate(ps, axis=1))
               * (1.0 / jnp.concatenate(ls, axis=1)))
        for p in range(B_REP // 2):
            pair = jnp.concatenate([o_t[:, (2 * p) * tq:(2 * p + 1) * tq], o_t[:, (2 * p + 1) * tq:(2 * p + 2) * tq]], axis=0)
            col0 = (kh * B_REP + 2 * p) * hd
            o_ref[:, col0:col0 + LANES] = pair.T.astype(o_ref.dtype)


def _swa_prompt(qt_all, kv_all, sinks, *, bsz, t_len, tq=256):
    nq = t_len // tq
    hq = B_HEADS * B_HEAD_DIM
    nk = B_KV_HEADS * B_HEAD_DIM
    assert t_len % tq == 0 and t_len >= B_WINDOW + tq and tq % LANES == 0 and B_WINDOW % LANES == 0
    return pl.pallas_call(
        functools.partial(_swa_prompt_body, tq=tq, t_len=t_len),
        grid=(bsz, nq),
        in_specs=[
            pl.BlockSpec(memory_space=pltpu.SMEM),
            pl.BlockSpec((hq, tq), lambda b, i: (0, b * nq + i)),
            pl.BlockSpec((t_len, kv_all.shape[1]), lambda b, i: (b, 0)),
        ],
        out_specs=pl.BlockSpec((tq, hq), lambda b, i: (b * nq + i, 0)),
        out_shape=jax.ShapeDtypeStruct((bsz * t_len, hq), _MXU_DTYPE),
        scratch_shapes=[pltpu.VMEM((t_len, nk), _MXU_DTYPE), pltpu.VMEM((t_len // LANES, nk, LANES), _MXU_DTYPE)],
        compiler_params=_cparams(("arbitrary", "arbitrary"), 40),
    )(sinks, qt_all, kv_all)


def _swa_sample_body(qbd_ref, ckv_ref, nkv_ref, lane_ref, o_ref, *, bt, past, wbuf, tk):
    hd = B_HEAD_DIM
    nk = B_KV_HEADS * hd
    scale2 = hd ** -0.5 * LOG2E
    lanes = qbd_ref.shape[2]
    slope2 = lane_ref[0:1, :]
    sink2 = lane_ref[1:2, :]
    qpos = past + lane_ref[2:3, :].astype(jnp.int32)
    kpos = past - wbuf + lax.broadcasted_iota(jnp.int32, (tk, 1), 0)
    dw = qpos - kpos
    distm = jnp.where((dw >= 0) & (dw < B_WINDOW), dw.astype(_F32), MASKED_DIST)
    lane_kh = _idiv(lax.broadcasted_iota(jnp.int32, (hd, lanes), 1), lanes // B_KV_HEADS)
    pad = jnp.zeros((tk - wbuf - nkv_ref.shape[1], 2 * nk), _F32)
    for bi in range(bt):
        kv = jnp.concatenate([ckv_ref[bi], nkv_ref[bi], pad], axis=0)
        s_t = _dot(kv[:, :nk].astype(_MXU_DTYPE), qbd_ref[bi])
        e, _, l = _softmax_cols(s_t, distm, slope2, scale2, sink2=sink2)
        p_t = (e * (1.0 / l)).astype(_MXU_DTYPE)
        o_full = _dot(kv[:, nk:].T.astype(_MXU_DTYPE), p_t)
        o = jnp.zeros((hd, lanes), _F32)
        for kh in range(B_KV_HEADS):
            o = jnp.where(lane_kh == kh, o_full[kh * hd:(kh + 1) * hd, :], o)
        o_ref[bi] = o


def _swa_sample(qbd, ckv, nkv, lane_consts, *, past, bt=16):
    dbsz, nk, lanes = qbd.shape
    wbuf = ckv.shape[1]
    tk = -(-(wbuf + nkv.shape[1]) // LANES) * LANES
    bt = _pick_tile(dbsz, bt)
    return pl.pallas_call(
        functools.partial(_swa_sample_body, bt=bt, past=past, wbuf=wbuf, tk=tk),
        grid=(dbsz // bt,),
        in_specs=[
            pl.BlockSpec((bt,) + qbd.shape[1:], lambda b: (b, 0, 0)),
            pl.BlockSpec((bt,) + ckv.shape[1:], lambda b: (b, 0, 0)),
            pl.BlockSpec((bt,) + nkv.shape[1:], lambda b: (b, 0, 0)),
            pl.BlockSpec(lane_consts.shape, lambda b: (0, 0)),
        ],
        out_specs=pl.BlockSpec((bt, B_HEAD_DIM, lanes), lambda b: (b, 0, 0)),
        out_shape=jax.ShapeDtypeStruct((dbsz, B_HEAD_DIM, lanes), _F32),
        compiler_params=_cparams(("arbitrary",), 40),
    )(qbd, ckv, nkv, lane_consts)


def _overlap(n_cmp, n_sel, rows, cols):
    start = np.arange(n_cmp) * CMP_STRIDE
    end = start + CMP_LEN - 1
    s0 = np.arange(n_sel) * SEL_BLOCK
    s1 = s0 + SEL_BLOCK - 1
    m = np.zeros((rows, cols), np.float32)
    m[:n_cmp, :n_sel] = (start[:, None] <= s1[None, :]) & (end[:, None] >= s0[None, :])
    return m


def _expand_mat_t(n_blk, t_len, chunk):
    key_blk = np.arange(t_len) // SEL_BLOCK
    m = (key_blk[:, None] == np.arange(n_blk)[None, :]).astype(np.float32)
    return jnp.asarray(m.reshape(t_len // chunk, chunk, n_blk), dtype=_MXU_DTYPE)


def kernel(x_prompt, x_sample, cache_nsa_kv, cache_nsa_win, cache_shared_win, page_table,
           norm_attn, norm_mlp, a_w_in, a_q_gain, a_k_gain, a_cmp_pe, a_cmp_w1, a_cmp_w2, a_w_out,
           kv_norm, kv_w, kv_k_gain, b_w_q, b_q_gain, b_sinks, b_w_out, mlp_w1, mlp_w2):
    bsz, t_len, dm = x_prompt.shape
    dbsz, t_new, _ = x_sample.shape
    n_a = a_w_in.shape[0]
    depth = norm_attn.shape[0]
    n_pool, page_len = cache_nsa_kv.shape[1:3]
    past = page_table.shape[1] * page_len
    d, g, rep = A_HEAD_DIM, A_KV_GROUPS, A_REP
    hq = A_HEADS * d
    hkv = 6 * g * d
    mp = bsz * t_len
    ms = dbsz * t_new
    wdt = _MXU_DTYPE
    slopes_a = jnp.asarray(_alibi_slopes(A_HEADS))

    hs2 = [x_prompt.reshape(mp, dm), x_sample.reshape(ms, dm)]
    both = lambda fn: [fn(hh) for hh in hs2]
    rows_p, rows_s, win_p, win_s = [], [], [], []
    kv_sh = None
    for layer in range(depth):
        if layer < n_a:
            a = layer
            w_in = a_w_in[a]
            wq = w_in[:, :hq].astype(wdt)
            qt_p, qt_s = both(lambda hh: _rms_mm(hh, norm_attn[layer], wq, epi="headnorm", transpose_out=True,
                                                 gain=jnp.tile(a_q_gain[a], A_HEADS), flag=jnp.ones((hq,), _F32),
                                                 out_dtype=wdt))
            one, zero = jnp.ones((g * d,), _F32), jnp.zeros((g * d,), _F32)
            kgain = a_k_gain[a]
            kv_gain = jnp.concatenate([one, one, jnp.tile(kgain[1], g), one, jnp.tile(kgain[2], g), one])
            kv_flag = jnp.concatenate([zero, zero, one, zero, one, zero])
            wkv = w_in[:, hq:hq + hkv].astype(wdt)
            kv_proj = functools.partial(_rms_mm, gamma=norm_attn[layer], w=wkv, epi="headnorm", gain=kv_gain, flag=kv_flag,
                                        split=(4 * g * d, 2 * g * d))
            rw_p, wn_p, rw_il = kv_proj(hs2[0], interleave=4 * g)
            rw_s, wn_s = kv_proj(hs2[1])
            n_gate = 3 * A_HEADS
            w_gate = jnp.pad(w_in[:, hq + hkv:], ((0, 0), (0, LANES - n_gate))).astype(wdt)
            gates_p, gates_s = both(lambda hh: _rms_mm(hh, norm_attn[layer], w_gate, epi="sigmoid")[:, :n_gate])

            w1cat = (a_cmp_w1[a].reshape(2, CMP_RATIO, CMP_STRIDE, d, d).transpose(0, 2, 3, 1, 4)
                     .reshape(2, CMP_STRIDE * d, CMP_RATIO * d).astype(wdt))
            pe2 = jnp.pad(a_cmp_pe[a].reshape(2, CMP_RATIO, CMP_STRIDE * d),
                          ((0, 0), (0, SUBLANES - CMP_RATIO), (0, 0))).astype(wdt)
            w2 = a_cmp_w2[a].astype(wdt)
            kgain0 = kgain[0].reshape(1, d)

            kc, vct = _cmp_prompt(rw_p, w1cat, pe2, w2, kgain0, bsz=bsz, t_len=t_len)
            n_chunk = t_len // CMP_STRIDE
            n_sel_p = -(-t_len // SEL_BLOCK)
            n_blk_p = -(-n_sel_p // SUBLANES) * SUBLANES
            ovt_p = jnp.asarray(_overlap(n_chunk - CMP_RATIO + 1, n_sel_p, n_chunk, n_blk_p).T)
            chunk = 512
            et = _expand_mat_t(n_blk_p, t_len, chunk)
            gates_t = (gates_p.reshape(bsz, t_len, 3, g, rep).transpose(0, 3, 2, 4, 1)
                       .reshape(bsz, g, 3 * rep, t_len))
            o_p = _nsa_prompt(qt_p, rw_p, wn_p, gates_t, kc, vct, ovt_p, et, slopes_a,
                              bsz=bsz, t_len=t_len, chunk=chunk)

            q_s = (qt_s.astype(_F32).reshape(g, rep, d, dbsz, t_new).transpose(3, 0, 4, 1, 2)
                   .reshape(dbsz, g, t_new * rep, d))
            gt_s = (gates_s.reshape(dbsz, t_new, 3, g, rep).transpose(0, 3, 1, 4, 2)
                    .reshape(dbsz, g, t_new * rep, 3))
            kv_s = jnp.concatenate([rw_s, wn_s], axis=1).reshape(dbsz, t_new, 6, g, d)
            new_rows = lambda slot: jnp.pad(kv_s[:, :, slot].transpose(0, 2, 1, 3),
                                            ((0, 0), (0, 0), (0, SUBLANES - t_new), (0, 0)))
            cache_rows = cache_nsa_kv[a].reshape(n_pool, page_len * 4 * g, d)
            cwin = cache_nsa_win[a]
            wbuf = cwin.shape[1]
            n_chunk_s = past // CMP_STRIDE
            n_sel_s = -(-(past + t_new) // SEL_BLOCK)
            n_blk_l = -(-n_sel_s // LANES) * LANES
            ov_s = jnp.asarray(_overlap(n_chunk_s - CMP_RATIO + 1, n_sel_s, n_chunk_s, n_blk_l))
            o_s = _nsa_sample(page_table, slopes_a, cache_rows, q_s, gt_s, new_rows(2), new_rows(3), new_rows(4),
                              new_rows(5), cwin.reshape(dbsz, wbuf * 2 * g, d), w1cat, pe2, w2, kgain0, ov_s,
                              past=past, t_new=t_new, page_len=page_len)
            o_s = o_s.reshape(dbsz, g, t_new, rep, d).transpose(0, 2, 1, 3, 4).reshape(ms, hq).astype(wdt)
            w_out = a_w_out[a]

            rows_p.append(rw_il.reshape(bsz, t_len, 4, g, d))
            rows_s.append(rw_s.reshape(dbsz, t_new, 4, g, d))
            win_p.append(wn_p.reshape(bsz, t_len, 2, g, d)[:, -min(A_WINDOW, t_len):])
            win_s.append(jnp.concatenate([cwin, wn_s.reshape(dbsz, t_new, 2, g, d)], axis=1)[:, -wbuf:])
        else:
            bl = layer - n_a
            hb = B_HEADS * B_HEAD_DIM
            nk = B_KV_HEADS * B_HEAD_DIM
            wq = b_w_q[bl].astype(wdt)
            qt_p, qt_s = both(lambda hh: _rms_mm(hh, norm_attn[layer], wq, epi="headnorm", hd=B_HEAD_DIM,
                                                 transpose_out=True, gain=jnp.tile(b_q_gain[bl], B_HEADS),
                                                 flag=jnp.ones((hb,), _F32), out_dtype=wdt))
            o_p = _swa_prompt(qt_p, kv_sh[0], b_sinks[bl], bsz=bsz, t_len=t_len)

            q5 = (qt_s.astype(_F32).reshape(B_KV_HEADS, B_REP, B_HEAD_DIM, dbsz, t_new)
                  .transpose(3, 0, 2, 1, 4).reshape(dbsz, B_KV_HEADS, B_HEAD_DIM, B_REP * t_new))
            qbd = (q5[:, :, :, None, :] * jnp.eye(B_KV_HEADS, dtype=_F32)[None, :, None, :, None]
                   ).reshape(dbsz, nk, B_HEADS * t_new).astype(wdt)
            slopes_b = jnp.asarray(_alibi_slopes(B_HEADS))
            lane_consts = jnp.stack([jnp.repeat(slopes_b * LOG2E, t_new), jnp.repeat(b_sinks[bl] * LOG2E, t_new),
                                     jnp.tile(jnp.arange(t_new, dtype=_F32), B_HEADS)]
                                    + [jnp.zeros((B_HEADS * t_new,), _F32)] * (SUBLANES - 3))
            nkv = jnp.pad(kv_sh[1].reshape(dbsz, t_new, 2 * nk), ((0, 0), (0, SUBLANES - t_new), (0, 0)))
            ckv = cache_shared_win.reshape(dbsz, cache_shared_win.shape[1], 2 * nk)
            o_s = _swa_sample(qbd, ckv, nkv, lane_consts, past=past)
            o_s = (o_s.reshape(dbsz, B_HEAD_DIM, B_HEADS, t_new).transpose(0, 3, 2, 1).reshape(ms, hb).astype(wdt))
            w_out = b_w_out[bl]

        w_o = w_out.astype(wdt)
        h_s, w_1, w_2 = _mlp_cast(_mm_res(o_s, w_o, hs2[1]), norm_mlp[layer], mlp_w1[layer], mlp_w2[layer])
        hs2 = [_mlp(_mm_res(o_p, w_o, hs2[0]), norm_mlp[layer], w_1, w_2), h_s]
        if layer == n_a - 1:
            nk = B_KV_HEADS * B_HEAD_DIM
            wkv_sh = kv_w.astype(wdt)
            kv_sh = both(lambda hh: _rms_mm(
                hh, kv_norm, wkv_sh, epi="headnorm", hd=B_HEAD_DIM,
                gain=jnp.concatenate([jnp.tile(kv_k_gain, B_KV_HEADS), jnp.ones((nk,), _F32)]),
                flag=jnp.concatenate([jnp.ones((nk,), _F32), jnp.zeros((nk,), _F32)])))

    wb = cache_shared_win.shape[1]
    kv_p = kv_sh[0].reshape(bsz, t_len, 2, B_KV_HEADS, B_HEAD_DIM)
    kv_s_new = kv_sh[1].reshape(dbsz, t_new, 2, B_KV_HEADS, B_HEAD_DIM)
    return (hs2[0].reshape(bsz, t_len, dm), hs2[1].reshape(dbsz, t_new, dm),
            jnp.stack(rows_p), jnp.stack(rows_s), jnp.stack(win_p), jnp.stack(win_s),
            kv_p[:, -min(B_WINDOW, t_len):], jnp.concatenate([cache_shared_win, kv_s_new], axis=1)[:, -wb:])
```

```python
import functools

import jax
import jax.numpy as jnp
import numpy as np
from jax import lax
from jax.experimental import pallas as pl
from jax.experimental.pallas import tpu as pltpu

A_HEADS = 16
A_HEAD_DIM = 128
A_KV_GROUPS = 2
A_REP = A_HEADS // A_KV_GROUPS
CMP_LEN = 32
CMP_STRIDE = 16
CMP_RATIO = CMP_LEN // CMP_STRIDE
SEL_BLOCK = 64
SEL_TOPK = 16
N_LOCAL_FORCED = 2
A_WINDOW = 512
B_HEADS = 32
B_HEAD_DIM = 64
B_KV_HEADS = 4
B_REP = B_HEADS // B_KV_HEADS
B_WINDOW = 128
NORM_EPS = 1e-6
NEG_INF = -1e30
FORCE_BONUS = 1e4

LANES = 128
SUBLANES = 8
X2_PITCH = SUBLANES + 1
PAGE_PARTS = 2

LOG2E = float(np.log2(np.e))
MASKED_DIST = 1e33

_MXU_DTYPE = jnp.bfloat16
_F32 = jnp.float32


def _cparams(sem, vmem_mb):
    return pltpu.CompilerParams(dimension_semantics=sem, vmem_limit_bytes=vmem_mb * 1024 * 1024)


def _dot(a, b):
    return jnp.dot(a, b, preferred_element_type=_F32)


def _dot_nt(a, b):
    return lax.dot_general(a, b, (((1,), (1,)), ((), ())), preferred_element_type=_F32)


def _split3(a):
    hi = a.astype(_MXU_DTYPE)
    r1 = a - hi.astype(_F32)
    mid = r1.astype(_MXU_DTYPE)
    lo = (r1 - mid.astype(_F32)).astype(_MXU_DTYPE)
    return hi, mid, lo


def _dot_exact(a, b01):
    bm = b01.astype(_MXU_DTYPE)
    hi, mid, lo = _split3(a)
    return _dot(hi, bm) + _dot(mid, bm) + _dot(lo, bm)


def _dot_exact_rhs(a01, b):
    am = a01.astype(_MXU_DTYPE)
    hi, mid, lo = _split3(b)
    return _dot(am, hi) + _dot(am, mid) + _dot(am, lo)


def _idiv(x, n):
    if n & (n - 1) == 0:
        return lax.shift_right_logical(x, jnp.int32(n.bit_length() - 1))
    return x // n


def _pick_tile(m, want):
    if m <= want:
        return m
    for t in range(want - want % SUBLANES, 0, -SUBLANES):
        if m % t == 0:
            return t
    raise ValueError(f"no tile for {m}")


def _alibi_slopes(n_heads):
    return np.asarray(2.0 ** (-8.0 * np.arange(1, n_heads + 1) / n_heads), dtype=np.float32)


def _rms_rows(x, gain):
    ms = jnp.mean(x * x, axis=-1, keepdims=True)
    return x * lax.rsqrt(ms + NORM_EPS) * gain


def _head_norm(y, gain, flag, hd):
    outs = []
    for c in range(y.shape[1] // LANES):
        yc = y[:, c * LANES:(c + 1) * LANES]
        y2 = yc * yc
        if hd == LANES:
            ms = jnp.mean(y2, axis=-1, keepdims=True)
        else:
            row = _idiv(lax.broadcasted_iota(jnp.int32, (LANES, LANES), 0), hd)
            col = _idiv(lax.broadcasted_iota(jnp.int32, (LANES, LANES), 1), hd)
            ms = _dot_exact(y2, (row == col).astype(_F32)) * (1.0 / hd)
        yn = yc * lax.rsqrt(ms + NORM_EPS) * gain[:, c * LANES:(c + 1) * LANES]
        outs.append(jnp.where(flag[:, c * LANES:(c + 1) * LANES] > 0, yn, yc))
    return outs[0] if len(outs) == 1 else jnp.concatenate(outs, axis=1)


def _rms_mm_body(x_ref, gam_ref, w_ref, gain_ref, flag_ref, *refs, epi, hd, transpose_out, interleave):
    o_refs, xn_ref = refs[:-1], refs[-1]
    if interleave:
        o_refs, il_ref = o_refs[:-1], o_refs[-1]

    @pl.when(pl.program_id(1) == 0)
    def _():
        xn_ref[...] = _rms_rows(x_ref[...], gam_ref[...]).astype(xn_ref.dtype)

    y = _dot(xn_ref[...], w_ref[...])
    if epi == "headnorm":
        y = _head_norm(y, gain_ref[...], flag_ref[...], hd)
    elif epi == "sigmoid":
        y = jax.nn.sigmoid(y)
    if transpose_out:
        y = y.T
    col = 0
    for o_ref in o_refs:
        o_ref[...] = y[:, col:col + o_ref.shape[1]].astype(o_ref.dtype)
        col += o_ref.shape[1]
    if interleave:
        for c in range(interleave):
            il_ref[pl.ds(c, y.shape[0], stride=interleave), :] = y[:, c * LANES:(c + 1) * LANES]


def _rms_mm(x, gamma, w, *, epi="none", gain=None, flag=None, hd=LANES, out_dtype=_F32, transpose_out=False,
            split=None, interleave=0, tm=512, tn=2048):
    m, k = x.shape
    n = w.shape[1]
    tm = _pick_tile(m, tm)
    tn = _pick_tile(n, tn)
    if gain is None:
        gain = jnp.ones((n,), _F32)
        flag = jnp.zeros((n,), _F32)
    if transpose_out:
        out_spec = pl.BlockSpec((tn, tm), lambda i, j: (j, i))
        out_shape = jax.ShapeDtypeStruct((n, m), out_dtype)
    elif split is not None:
        assert tn == n and sum(split) == n
        out_spec = [pl.BlockSpec((tm, ns), lambda i, j: (i, 0)) for ns in split]
        out_shape = [jax.ShapeDtypeStruct((m, ns), out_dtype) for ns in split]
        if interleave:
            out_spec.append(pl.BlockSpec((tm * interleave, LANES), lambda i, j: (i, 0)))
            out_shape.append(jax.ShapeDtypeStruct((m * interleave, LANES), out_dtype))
    else:
        out_spec = pl.BlockSpec((tm, tn), lambda i, j: (i, j))
        out_shape = jax.ShapeDtypeStruct((m, n), out_dtype)
    return pl.pallas_call(
        functools.partial(_rms_mm_body, epi=epi, hd=hd, transpose_out=transpose_out, interleave=interleave),
        grid=(m // tm, n // tn),
        in_specs=[
            pl.BlockSpec((tm, k), lambda i, j: (i, 0)),
            pl.BlockSpec((1, k), lambda i, j: (0, 0)),
            pl.BlockSpec((k, tn), lambda i, j: (0, j)),
            pl.BlockSpec((1, tn), lambda i, j: (0, j)),
            pl.BlockSpec((1, tn), lambda i, j: (0, j)),
        ],
        out_specs=out_spec,
        out_shape=out_shape,
        scratch_shapes=[pltpu.VMEM((tm, k), _MXU_DTYPE)],
        compiler_params=_cparams(("arbitrary", "arbitrary"), 56),
    )(x, gamma.reshape(1, k), w, gain.reshape(1, n), flag.reshape(1, n))


def _mm_res_body(a_ref, w_ref, h_ref, o_ref):
    o_ref[...] = h_ref[...] + _dot(a_ref[...], w_ref[...])


def _mm_res(a, w, h, *, tm=512, tn=2048):
    m, k = a.shape
    n = w.shape[1]
    tm = _pick_tile(m, tm)
    tn = _pick_tile(n, tn)
    return pl.pallas_call(
        _mm_res_body,
        grid=(m // tm, n // tn),
        in_specs=[
            pl.BlockSpec((tm, k), lambda i, j: (i, 0)),
            pl.BlockSpec((k, tn), lambda i, j: (0, j)),
            pl.BlockSpec((tm, tn), lambda i, j: (i, j)),
        ],
        out_specs=pl.BlockSpec((tm, tn), lambda i, j: (i, j)),
        out_shape=jax.ShapeDtypeStruct((m, n), _F32),
        compiler_params=_cparams(("arbitrary", "arbitrary"), 56),
    )(a, w, h)


def _mlp_body(h_ref, gam_ref, w1_ref, w2_ref, o_ref, xn_ref):
    @pl.when(pl.program_id(1) == 0)
    def _():
        h = h_ref[...]
        xn_ref[...] = _rms_rows(h, gam_ref[...]).astype(xn_ref.dtype)
        o_ref[...] = h

    a = jnp.square(jnp.maximum(_dot(xn_ref[...], w1_ref[...]), 0.0))
    o_ref[...] += _dot(a.astype(_MXU_DTYPE), w2_ref[...])


def _mlp_cast_body(h_ref, gam_ref, w1_ref, w2_ref, o_ref, w1b_ref, w2b_ref, xn_ref):
    @pl.when(pl.program_id(0) == 0)
    def _():
        h = h_ref[...]
        xn_ref[...] = _rms_rows(h, gam_ref[...]).astype(xn_ref.dtype)
        o_ref[...] = h

    w1b = w1_ref[...].astype(w1b_ref.dtype)
    w2b = w2_ref[...].astype(w2b_ref.dtype)
    w1b_ref[...] = w1b
    w2b_ref[...] = w2b
    a = jnp.square(jnp.maximum(_dot(xn_ref[...], w1b), 0.0))
    o_ref[...] += _dot(a.astype(_MXU_DTYPE), w2b)


def _mlp_cast(h, gamma, w1, w2, *, tf=512):
    m, d = h.shape
    f = w1.shape[1]
    tf = _pick_tile(f, tf)
    return pl.pallas_call(
        _mlp_cast_body,
        grid=(f // tf,),
        in_specs=[
            pl.BlockSpec((m, d), lambda j: (0, 0)),
            pl.BlockSpec((1, d), lambda j: (0, 0)),
            pl.BlockSpec((d, tf), lambda j: (0, j)),
            pl.BlockSpec((tf, d), lambda j: (j, 0)),
        ],
        out_specs=[pl.BlockSpec((m, d), lambda j: (0, 0)),
                   pl.BlockSpec((d, tf), lambda j: (0, j)),
                   pl.BlockSpec((tf, d), lambda j: (j, 0))],
        out_shape=[jax.ShapeDtypeStruct((m, d), _F32), jax.ShapeDtypeStruct(w1.shape, _MXU_DTYPE),
                   jax.ShapeDtypeStruct(w2.shape, _MXU_DTYPE)],
        scratch_shapes=[pltpu.VMEM((m, d), _MXU_DTYPE)],
        compiler_params=_cparams(("arbitrary",), 56),
    )(h, gamma.reshape(1, d), w1, w2)


def _mlp(h, gamma, w1, w2, *, tm=1024, tf=512):
    m, d = h.shape
    f = w1.shape[1]
    tm = _pick_tile(m, tm)
    tf = _pick_tile(f, tf)
    return pl.pallas_call(
        _mlp_body,
        grid=(m // tm, f // tf),
        in_specs=[
            pl.BlockSpec((tm, d), lambda i, j: (i, 0)),
            pl.BlockSpec((1, d), lambda i, j: (0, 0)),
            pl.BlockSpec((d, tf), lambda i, j: (0, j)),
            pl.BlockSpec((tf, d), lambda i, j: (j, 0)),
        ],
        out_specs=pl.BlockSpec((tm, d), lambda i, j: (i, 0)),
        out_shape=jax.ShapeDtypeStruct((m, d), _F32),
        scratch_shapes=[pltpu.VMEM((tm, d), _MXU_DTYPE)],
        compiler_params=_cparams(("arbitrary", "arbitrary"), 56),
    )(h, gamma.reshape(1, d), w1, w2)


def _block_scores(imp, blk, cur, n_sel):
    score, _ = _block_scores_forced(imp, blk, cur, n_sel)
    return score


def _block_scores_forced(imp, blk, cur, n_sel):
    valid = blk <= cur
    forced = (blk == 0) | (((cur - blk) < N_LOCAL_FORCED) & valid)
    score = jnp.where(valid, imp + FORCE_BONUS * forced.astype(_F32), NEG_INF)
    return jnp.where(blk < n_sel, score, -jnp.inf), forced


def _select_blocks_rows(imp, qpos, n_sel, between=()):
    assert FORCE_BONUS > 2 * A_REP
    n_forced = 1 + N_LOCAL_FORCED
    blk = lax.broadcasted_iota(jnp.int32, imp.shape, 1)
    s, forced = _block_scores_forced(imp, blk, _idiv(qpos, SEL_BLOCK), n_sel)
    sel = jnp.where(forced, 1.0, 0.0)
    s = jnp.where(forced, -jnp.inf, s)
    pending = list(between)
    for _ in range(min(SEL_TOPK, n_sel) - n_forced):
        hit = blk == jnp.argmax(s, axis=-1, keepdims=True).astype(jnp.int32)
        sel = jnp.where(hit, 1.0, sel)
        s = jnp.where(hit, -jnp.inf, s)
        if pending:
            pending.pop(0)()
    for work in pending:
        work()
    return sel


def _select_blocks_cols(imp_t, qpos, n_sel):
    blk = lax.broadcasted_iota(jnp.int32, imp_t.shape, 0)
    s = _block_scores(imp_t, blk, _idiv(qpos, SEL_BLOCK), n_sel)
    beaten = jnp.zeros(imp_t.shape, _F32)
    for j in range(n_sel):
        sj = s[j:j + 1, :]
        tie = jnp.where(blk > j, 1.0, 0.0)
        beaten = beaten + jnp.where(sj > s, 1.0, 0.0) + jnp.where(sj == s, tie, 0.0)
    return jnp.where((beaten < min(SEL_TOPK, n_sel)) & (blk < n_sel), 1.0, 0.0)


def _compress_x2(x2, w1cat, pe2, w2, n_chunk):
    return _compress_finish(_dot(x2.astype(_MXU_DTYPE), w1cat), w1cat, pe2, w2, n_chunk)


def _compress_finish(h, w1cat, pe2, w2, n_chunk):
    d = A_HEAD_DIM
    c = _dot(pe2, w1cat)
    part0 = h[:, :d] + c[0:1, :d]
    part1 = h[:, d:] + c[1:2, d:]
    pieces = []
    for s in range(h.shape[0] // n_chunk):
        p1 = part1[s * n_chunk:(s + 1) * n_chunk]
        pieces.append(part0[s * n_chunk:(s + 1) * n_chunk] + pltpu.roll(p1, n_chunk - 1, 0))
    hh = pieces[0] if len(pieces) == 1 else jnp.concatenate(pieces, axis=0)
    return _dot(jax.nn.gelu(hh).astype(_MXU_DTYPE), w2)


def _softmax_cols(s_t, distm, slope2, scale2, sink2=None):
    sc = s_t * scale2 - slope2 * distm
    m = jnp.max(sc, axis=0, keepdims=True)
    if sink2 is not None:
        m = jnp.maximum(m, sink2)
    e = jnp.exp2(sc - m)
    l = jnp.sum(e, axis=0, keepdims=True)
    if sink2 is not None:
        l = l + jnp.exp2(sink2 - m)
    return e, m, l


def _cmp_prompt_body(kraw_ref, vraw_ref, w1_ref, pe_ref, w2_ref, kg_ref, kc_ref, vct_ref, *, n_chunk):
    for slot, src in enumerate((kraw_ref, vraw_ref)):
        x2 = jnp.concatenate([src[pl.ds(s, n_chunk, stride=CMP_STRIDE), :] for s in range(CMP_STRIDE)], axis=1)
        y = _compress_x2(x2, w1_ref[slot], pe_ref[slot], w2_ref[slot], n_chunk)
        if slot == 0:
            kc_ref[...] = _rms_rows(y, kg_ref[...])
        else:
            vct_ref[...] = y.T


def _cmp_prompt(kv, w1cat, pe2, w2, kgain0, *, bsz, t_len):
    n_chunk = t_len // CMP_STRIDE
    d = A_HEAD_DIM
    g = A_KV_GROUPS
    return pl.pallas_call(
        functools.partial(_cmp_prompt_body, n_chunk=n_chunk),
        grid=(bsz, g),
        in_specs=[
            pl.BlockSpec((t_len, d), lambda b, gi: (b, gi)),
            pl.BlockSpec((t_len, d), lambda b, gi: (b, g + gi)),
            pl.BlockSpec(w1cat.shape, lambda b, gi: (0, 0, 0)),
            pl.BlockSpec(pe2.shape, lambda b, gi: (0, 0, 0)),
            pl.BlockSpec(w2.shape, lambda b, gi: (0, 0, 0)),
            pl.BlockSpec((1, d), lambda b, gi: (0, 0)),
        ],
        out_specs=[pl.BlockSpec((None, None, n_chunk, d), lambda b, gi: (b, gi, 0, 0)),
                   pl.BlockSpec((None, None, d, n_chunk), lambda b, gi: (b, gi, 0, 0))],
        out_shape=[jax.ShapeDtypeStruct((bsz, g, n_chunk, d), _F32), jax.ShapeDtypeStruct((bsz, g, d, n_chunk), _F32)],
        compiler_params=_cparams(("arbitrary", "arbitrary"), 32),
    )(kv, kv, w1cat, pe2, w2, kgain0)


def _nsa_prompt_body(slope_ref, qt_ref, gt_ref, kc_ref, vct_ref, ks_ref, vs_ref, kw_ref, vw_ref, ovt_ref, et_ref,
                     o_ref, ksb, vst, kwb, vwt, *, tq, t_len, n_cmp, n_sel, chunk):
    gi = pl.program_id(1)
    i = pl.program_id(2)
    d = A_HEAD_DIM
    rep = A_REP
    scale2 = d ** -0.5 * LOG2E
    n_chunks = t_len // chunk
    n_wblk = A_WINDOW // tq + 1

    @pl.when(i == 0)
    def _():
        ksb[...] = ks_ref[...].astype(ksb.dtype)
        kwb[...] = kw_ref[...].astype(kwb.dtype)
        for c in range(n_chunks):
            vst[c] = vs_ref[c * chunk:(c + 1) * chunk, :].T.astype(vst.dtype)
        for c in range(t_len // tq):
            vwt[c] = vw_ref[c * tq:(c + 1) * tq, :].T.astype(vwt.dtype)

    t0 = i * tq
    qt = qt_ref[...]
    qst = jnp.concatenate([qt[r * d:(r + 1) * d, :] for r in range(rep)], axis=1)
    tpos = t0 + lax.broadcasted_iota(jnp.int32, (1, tq), 1)
    slope2 = [slope_ref[gi * rep + r] * LOG2E for r in range(rep)]
    head = lambda a, r: a[:, r * tq:(r + 1) * tq]

    def masked_dist(valid, dist):
        return jnp.where(valid, dist.astype(_F32), MASKED_DIST)

    n_c = kc_ref.shape[0]
    cidx = lax.broadcasted_iota(jnp.int32, (n_c, 1), 0)
    cend = cidx * CMP_STRIDE + (CMP_LEN - 1)
    distm = masked_dist((cend <= tpos) & (cidx < n_cmp), tpos - cend)
    s_t = _dot(kc_ref[...].astype(_MXU_DTYPE), qst)
    any_valid = (tpos >= CMP_LEN - 1).astype(_F32)
    p_heads = []
    for r in range(rep):
        e, _, l = _softmax_cols(head(s_t, r), distm, slope2[r], scale2)
        p_heads.append(e * (any_valid / l))
    p_sum = p_heads[0]
    for r in range(1, rep):
        p_sum = p_sum + p_heads[r]
    o_cmp = _dot(vct_ref[...].astype(_MXU_DTYPE), jnp.concatenate(p_heads, axis=1).astype(_MXU_DTYPE))
    imp_t = _dot_exact_rhs(ovt_ref[...], p_sum)
    sel_t = _select_blocks_cols(imp_t, tpos, n_sel).astype(_MXU_DTYPE)

    n_live = (t0 + tq + chunk - 1) // chunk

    def sel_step(c, carry):
        m_i, l_i, acc = carry
        base = pl.multiple_of(c * chunk, chunk)
        kpos = base + lax.broadcasted_iota(jnp.int32, (chunk, 1), 0)
        picked = _dot(et_ref[c], sel_t)
        distm = masked_dist((picked > 0.5) & (kpos <= tpos), tpos - kpos)
        s_t = _dot(ksb[pl.ds(base, chunk), :], qst)
        es, ms, ls = [], [], []
        for r in range(rep):
            sc = head(s_t, r) * scale2 - slope2[r] * distm
            m_new = jnp.maximum(head(m_i, r), jnp.max(sc, axis=0, keepdims=True))
            e = jnp.exp2(sc - m_new)
            es.append(e.astype(_MXU_DTYPE))
            ms.append(m_new)
            ls.append(jnp.sum(e, axis=0, keepdims=True))
        m_new = jnp.concatenate(ms, axis=1)
        alpha = jnp.exp2(m_i - m_new)
        l_new = alpha * l_i + jnp.concatenate(ls, axis=1)
        acc = alpha * acc + _dot(vst[c], jnp.concatenate(es, axis=1))
        return m_new, l_new, acc

    init = (jnp.full((1, rep * tq), NEG_INF, _F32), jnp.zeros((1, rep * tq), _F32), jnp.zeros((d, rep * tq), _F32))
    _, l_s, acc_s = lax.fori_loop(0, n_live, sel_step, init)
    o_sel = acc_s * (1.0 / l_s)

    wlen = n_wblk * tq
    wb0 = jnp.maximum(i - (n_wblk - 1), 0)
    start = pl.multiple_of(wb0 * tq, tq)
    kpos = start + lax.broadcasted_iota(jnp.int32, (wlen, 1), 0)
    dw = tpos - kpos
    distm = masked_dist((dw >= 0) & (dw < A_WINDOW), dw)
    s_t = _dot(kwb[pl.ds(start, wlen), :], qst)
    es, ls = [], []
    for r in range(rep):
        e, _, l = _softmax_cols(head(s_t, r), distm, slope2[r], scale2)
        es.append(e.astype(_MXU_DTYPE))
        ls.append(l)
    vw = jnp.concatenate([vwt[wb0 + j] for j in range(n_wblk)], axis=1)
    o_win = _dot(vw, jnp.concatenate(es, axis=1)) * (1.0 / jnp.concatenate(ls, axis=1))

    gt = gt_ref[...]
    for r in range(rep):
        o_r = (gt[r:r + 1] * head(o_cmp, r) + gt[rep + r:rep + r + 1] * head(o_sel, r)
               + gt[2 * rep + r:2 * rep + r + 1] * head(o_win, r))
        o_ref[:, r * d:(r + 1) * d] = o_r.T.astype(o_ref.dtype)


def _nsa_prompt(qt_all, rows, win, gates_t, kc, vct, ovt, et, slopes, *, bsz, t_len, tq=256, chunk=512):
    d = A_HEAD_DIM
    g = A_KV_GROUPS
    rep = A_REP
    nq = t_len // tq
    n_chunk = kc.shape[2]
    n_cmp = n_chunk - CMP_RATIO + 1
    n_sel = -(-t_len // SEL_BLOCK)
    assert t_len % tq == 0 and A_WINDOW % tq == 0 and t_len >= A_WINDOW + tq
    assert t_len % chunk == 0 and chunk % SEL_BLOCK == 0
    kvspec = lambda col: pl.BlockSpec((t_len, d), lambda b, gi, i, col=col: (b, col + gi))
    return pl.pallas_call(
        functools.partial(_nsa_prompt_body, tq=tq, t_len=t_len, n_cmp=n_cmp, n_sel=n_sel, chunk=chunk),
        grid=(bsz, g, nq),
        in_specs=[
            pl.BlockSpec(memory_space=pltpu.SMEM),
            pl.BlockSpec((rep * d, tq), lambda b, gi, i: (gi, b * nq + i)),
            pl.BlockSpec((None, None, 3 * rep, tq), lambda b, gi, i: (b, gi, 0, i)),
            pl.BlockSpec((None, None, n_chunk, d), lambda b, gi, i: (b, gi, 0, 0)),
            pl.BlockSpec((None, None, d, n_chunk), lambda b, gi, i: (b, gi, 0, 0)),
            kvspec(2 * g), kvspec(3 * g), kvspec(0), kvspec(g),
            pl.BlockSpec(ovt.shape, lambda b, gi, i: (0, 0)),
            pl.BlockSpec(et.shape, lambda b, gi, i: (0, 0, 0)),
        ],
        out_specs=pl.BlockSpec((tq, rep * d), lambda b, gi, i: (b * nq + i, gi)),
        out_shape=jax.ShapeDtypeStruct((bsz * t_len, g * rep * d), _MXU_DTYPE),
        scratch_shapes=[pltpu.VMEM((t_len, d), _MXU_DTYPE), pltpu.VMEM((t_len // chunk, d, chunk), _MXU_DTYPE),
                        pltpu.VMEM((t_len, d), _MXU_DTYPE), pltpu.VMEM((t_len // tq, d, tq), _MXU_DTYPE)],
        compiler_params=_cparams(("arbitrary", "arbitrary", "arbitrary"), 48),
    )(slopes, qt_all, gates_t, kc, vct, rows, rows, win, win, ovt, et)


def _nsa_sample_body(pt_ref, slope_ref, *refs, pps, past, t_new, n_cmp, n_sel):
    pages = refs[:pps * PAGE_PARTS]
    (q_ref, gt_ref, ksn_ref, vsn_ref, kwn_ref, vwn_ref, cwin_ref, w1_ref, pe_ref, w2_ref, kg_ref, ov_ref,
     o_ref, x2, hcmp, ksb, vsb, emat) = refs[pps * PAGE_PARTS:]
    b = pl.program_id(0)
    pp = pl.program_id(1)
    n_pp = pl.num_programs(1)
    d = A_HEAD_DIM
    g = A_KV_GROUPS
    rep = A_REP
    rows = t_new * rep
    scale = d ** -0.5
    sg_stride = 4 * g
    part_len = pages[0].shape[0] // sg_stride
    page_len = part_len * PAGE_PARTS
    cpp = page_len // CMP_STRIDE
    assert cpp == SUBLANES and part_len % CMP_STRIDE == 0
    n_chunk = past // CMP_STRIDE
    n_rb = n_chunk // cpp

    @pl.when((b == 0) & (pp == 0))
    def _():
        blk = lax.broadcasted_iota(jnp.int32, emat.shape, 0)
        key = _idiv(lax.broadcasted_iota(jnp.int32, emat.shape, 1), SEL_BLOCK)
        emat[...] = (blk == key).astype(emat.dtype)
        x2[...] = jnp.zeros(x2.shape, x2.dtype)

    step_chunks = pps * cpp
    par = pp % 2

    def step_matmul(buf, step):
        for slot in range(2):
            slabs = [jnp.concatenate([x2[buf, slot, rb, pl.ds(s * X2_PITCH, cpp), :] for s in range(CMP_STRIDE)], axis=1)
                     for rb in range(g * pps)]
            h_step = _dot(jnp.concatenate(slabs, axis=0).astype(_MXU_DTYPE), w1_ref[slot])
            for gi in range(g):
                hrow = pl.multiple_of(gi * n_chunk + step * step_chunks, step_chunks)
                hcmp[slot, pl.ds(hrow, step_chunks), :] = h_step[gi * step_chunks:(gi + 1) * step_chunks]

    step_matmul(1 - par, (pp + n_pp - 1) % n_pp)
    for pi in range(pps):
        page_no = pp * pps + pi
        for part in range(PAGE_PARTS):
            pg = pages[pi * PAGE_PARTS + part]
            for gi in range(g):
                for slot in range(2):
                    a = pg[pl.ds(slot * g + gi, part_len, stride=sg_stride), :]
                    for cl in range(part_len // CMP_STRIDE):
                        c = part * (part_len // CMP_STRIDE) + cl
                        for hs in range(CMP_STRIDE // SUBLANES):
                            p0 = cl * CMP_STRIDE + hs * SUBLANES
                            x2[par, slot, gi * pps + pi,
                               pl.ds(hs * SUBLANES * X2_PITCH + c, SUBLANES, stride=X2_PITCH), :] = a[p0:p0 + SUBLANES]
                krow = pl.multiple_of(page_no * page_len + part * part_len, part_len)
                ksb[gi, pl.ds(krow, part_len), :] = pg[pl.ds(2 * g + gi, part_len, stride=sg_stride), :].astype(ksb.dtype)
                vsb[gi, pl.ds(krow, part_len), :] = pg[pl.ds(3 * g + gi, part_len, stride=sg_stride), :].astype(vsb.dtype)

    @pl.when(pp == n_pp - 1)
    def _():
        step_matmul(par, pp)
        kc_all = _rms_rows(_compress_finish(hcmp[0], w1_ref[0], pe_ref[0], w2_ref[0], n_chunk), kg_ref[...])
        vc_all = _compress_finish(hcmp[1], w1_ref[1], pe_ref[1], w2_ref[1], n_chunk)
        rowi = lax.broadcasted_iota(jnp.int32, (rows, 1), 0)
        qpos = past + _idiv(rowi, rep)
        head_r = rowi - _idiv(rowi, rep) * rep
        pad_new = jnp.zeros((LANES - ksn_ref.shape[1], d), _F32)

        def padded(ref, gi):
            return jnp.concatenate([ref[gi], pad_new], axis=0).astype(_MXU_DTYPE)

        def biased(s, valid, dist, slope):
            return jnp.where(valid, s * scale - slope * dist, NEG_INF)

        def two_part(s_p, s_n, v_p, v_n):
            m = jnp.maximum(jnp.max(s_p, axis=-1, keepdims=True), jnp.max(s_n, axis=-1, keepdims=True))
            e_p = jnp.exp(s_p - m)
            e_n = jnp.exp(s_n - m)
            l = jnp.sum(e_p, axis=-1, keepdims=True) + jnp.sum(e_n, axis=-1, keepdims=True)
            return (_dot(e_p.astype(_MXU_DTYPE), v_p) + _dot(e_n.astype(_MXU_DTYPE), v_n)) / l

        qs, slopes, p_cs, imps = [], [], [], []
        cidx = lax.broadcasted_iota(jnp.int32, (1, n_chunk), 1)
        cend = cidx * CMP_STRIDE + (CMP_LEN - 1)
        for gi in range(g):
            q = q_ref[gi].astype(_MXU_DTYPE)
            slope = jnp.zeros((rows, 1), _F32)
            for r in range(rep):
                slope = jnp.where(head_r == r, slope_ref[gi * rep + r], slope)
            kc = kc_all[gi * n_chunk:(gi + 1) * n_chunk].astype(_MXU_DTYPE)
            s = biased(_dot_nt(q, kc), (cend <= qpos) & (cidx < n_cmp), (qpos - cend).astype(_F32), slope)
            e = jnp.exp(s - jnp.max(s, axis=-1, keepdims=True))
            p_c = e / jnp.sum(e, axis=-1, keepdims=True) * (qpos >= CMP_LEN - 1).astype(_F32)
            p3 = p_c.reshape(t_new, rep, n_chunk)
            p_sum = jnp.broadcast_to(jnp.sum(p3, axis=1, keepdims=True), p3.shape).reshape(rows, n_chunk)
            imps.append(_dot_exact(p_sum, ov_ref[...]))
            p_cs.append(p_c.astype(_MXU_DTYPE))
            qs.append(q)
            slopes.append(slope)

        npos = past + lax.broadcasted_iota(jnp.int32, (1, LANES), 1)
        wbuf = cwin_ref.shape[0] // (2 * g)
        wpos = past - wbuf + lax.broadcasted_iota(jnp.int32, (1, wbuf), 1)
        o_wins, s_raw, s_raw_new = [None] * g, [None] * g, [None] * g

        def selected_scores(gi):
            s_raw[gi] = _dot_nt(qs[gi], ksb[gi])
            s_raw_new[gi] = _dot_nt(qs[gi], padded(ksn_ref, gi))

        def window_branch(gi):
            q, slope = qs[gi], slopes[gi]
            kw = cwin_ref[pl.ds(gi, wbuf, stride=2 * g), :].astype(_MXU_DTYPE)
            vw = cwin_ref[pl.ds(g + gi, wbuf, stride=2 * g), :].astype(_MXU_DTYPE)
            dw = qpos - wpos
            s_p = biased(_dot_nt(q, kw), (dw >= 0) & (dw < A_WINDOW), dw.astype(_F32), slope)
            dn = qpos - npos
            s_n = biased(_dot_nt(q, padded(kwn_ref, gi)), (dn >= 0) & (dn < A_WINDOW), dn.astype(_F32), slope)
            o_wins[gi] = two_part(s_p, s_n, vw, padded(vwn_ref, gi))

        fillers = [functools.partial(fn, gi) for gi in range(g) for fn in (selected_scores, window_branch)]
        selm_all = _select_blocks_rows(jnp.concatenate(imps, axis=0), jnp.concatenate([qpos] * g, axis=0), n_sel,
                                       between=fillers)

        kpos = lax.broadcasted_iota(jnp.int32, (1, past), 1)
        new_blk = past // SEL_BLOCK
        for gi in range(g):
            q, slope = qs[gi], slopes[gi]
            selm = selm_all[gi * rows:(gi + 1) * rows]
            o_cmp = _dot(p_cs[gi], vc_all[gi * n_chunk:(gi + 1) * n_chunk].astype(_MXU_DTYPE))
            picked = _dot(selm.astype(_MXU_DTYPE), emat[...])
            s_p = biased(s_raw[gi], (picked > 0.5) & (kpos <= qpos), (qpos - kpos).astype(_F32), slope)
            sel_new = selm[:, new_blk:new_blk + 1] > 0.5
            s_n = biased(s_raw_new[gi], sel_new & (npos <= qpos), (qpos - npos).astype(_F32), slope)
            o_sel = two_part(s_p, s_n, vsb[gi], padded(vsn_ref, gi))

            gt = gt_ref[gi]
            o_ref[gi] = gt[:, 0:1] * o_cmp + gt[:, 1:2] * o_sel + gt[:, 2:3] * o_wins[gi]


def _nsa_sample(page_table, slopes, cache_rows, q_s, gt_s, ksn, vsn, kwn, vwn, cwin, w1cat, pe2, w2, kgain0, ov_s,
                *, past, t_new, page_len, pps=16):
    dbsz, n_pages = page_table.shape
    d = A_HEAD_DIM
    g = A_KV_GROUPS
    rows = t_new * A_REP
    n_chunk = past // CMP_STRIDE
    n_cmp = n_chunk - CMP_RATIO + 1
    n_sel = -(-(past + t_new) // SEL_BLOCK)
    n_blk_l = ov_s.shape[1]
    assert past % CMP_STRIDE == 0 and t_new < CMP_STRIDE and n_pages % pps == 0 and past % SEL_BLOCK == 0
    assert t_new <= SEL_BLOCK and n_sel <= n_blk_l
    assert past // SEL_BLOCK >= N_LOCAL_FORCED and min(SEL_TOPK, n_sel) >= 1 + N_LOCAL_FORCED
    page_rows = page_len * 4 * g
    cpp = page_len // CMP_STRIDE

    def page_spec(pi, part):
        return pl.BlockSpec((None, page_rows // PAGE_PARTS, d),
                            lambda b, pp, pt, pi=pi, part=part: (pt[b * n_pages + pp * pps + pi], part, 0))

    per_b = lambda shape: pl.BlockSpec((None,) + shape, lambda b, pp, pt: (b,) + (0,) * len(shape))
    whole = lambda a: pl.BlockSpec(a.shape, lambda b, pp, pt: (0,) * a.ndim)
    grid_spec = pltpu.PrefetchScalarGridSpec(
        num_scalar_prefetch=1,
        grid=(dbsz, n_pages // pps),
        in_specs=[pl.BlockSpec(memory_space=pltpu.SMEM)]
        + [page_spec(pi, part) for pi in range(pps) for part in range(PAGE_PARTS)] + [
            per_b((g, rows, d)), per_b((g, rows, 3)),
            per_b(ksn.shape[1:]), per_b(vsn.shape[1:]), per_b(kwn.shape[1:]), per_b(vwn.shape[1:]),
            per_b(cwin.shape[1:]),
            whole(w1cat), whole(pe2), whole(w2), whole(kgain0), whole(ov_s),
        ],
        out_specs=per_b((g, rows, d)),
        scratch_shapes=[
            pltpu.VMEM((2, 2, g * pps, CMP_STRIDE * X2_PITCH, d), _F32), pltpu.VMEM((2, g * n_chunk, CMP_RATIO * d), _F32),
            pltpu.VMEM((g, past, d), _MXU_DTYPE), pltpu.VMEM((g, past, d), _MXU_DTYPE),
            pltpu.VMEM((n_blk_l, past), _MXU_DTYPE),
        ],
    )
    return pl.pallas_call(
        functools.partial(_nsa_sample_body, pps=pps, past=past, t_new=t_new, n_cmp=n_cmp, n_sel=n_sel),
        grid_spec=grid_spec,
        out_shape=jax.ShapeDtypeStruct((dbsz, g, rows, d), _F32),
        compiler_params=_cparams(("arbitrary", "arbitrary"), 60),
    )(page_table.reshape(-1), slopes, *([cache_rows] * (pps * PAGE_PARTS)), q_s, gt_s, ksn, vsn, kwn, vwn, cwin,
      w1cat, pe2, w2, kgain0, ov_s)


def _swa_prompt_body(sink_ref, qt_ref, kv_ref, o_ref, kb, vt, *, tq, t_len):
    i = pl.program_id(1)
    hd = B_HEAD_DIM
    nk = B_KV_HEADS * hd
    scale2 = hd ** -0.5 * LOG2E
    slopes = _alibi_slopes(B_HEADS)
    n_wblk = (B_WINDOW + tq) // LANES
    wlen = n_wblk * LANES

    @pl.when(i == 0)
    def _():
        kb[...] = kv_ref[:, :nk].astype(kb.dtype)
        for c in range(t_len // LANES):
            vt[c] = kv_ref[c * LANES:(c + 1) * LANES, nk:].T.astype(vt.dtype)

    t0 = i * tq
    wb0 = jnp.maximum(i * (tq // LANES) - B_WINDOW // LANES, 0)
    start = pl.multiple_of(wb0 * LANES, LANES)
    tpos = t0 + lax.broadcasted_iota(jnp.int32, (1, tq), 1)
    kpos = start + lax.broadcasted_iota(jnp.int32, (wlen, 1), 0)
    dw = tpos - kpos
    distm = jnp.where((dw >= 0) & (dw < B_WINDOW), dw.astype(_F32), MASKED_DIST)
    kwin = kb[pl.ds(start, wlen), :]
    vwin_t = jnp.concatenate([vt[wb0 + j] for j in range(n_wblk)], axis=1)
    qt = qt_ref[...]
    zeros = jnp.zeros((hd, tq), qt.dtype)
    for kh in range(B_KV_HEADS):
        pb = (kh // 2) * LANES
        blocks = []
        for r in range(B_REP):
            h = kh * B_REP + r
            qh = qt[h * hd:(h + 1) * hd, :]
            blocks.append(jnp.concatenate([qh, zeros] if kh % 2 == 0 else [zeros, qh], axis=0))
        s_t = _dot(kwin[:, pb:pb + LANES], jnp.concatenate(blocks, axis=1))
        ps, ls = [], []
        for r in range(B_REP):
            h = kh * B_REP + r
            e, _, l = _softmax_cols(s_t[:, r * tq:(r + 1) * tq], distm, float(slopes[h]) * LOG2E, scale2,
                                    sink2=sink_ref[h] * LOG2E)
            ps.append(e.astype(_MXU_DTYPE))
            ls.append(l)
        o_t = (_dot(vwin_t[kh * hd:(kh + 1) * hd, :], jnp.concatenate(ps, axis=1))
               * (1.0 / jnp.concatenate(ls, axis=1)))
        for p in range(B_REP // 2):
            pair = jnp.concatenate([o_t[:, (2 * p) * tq:(2 * p + 1) * tq], o_t[:, (2 * p + 1) * tq:(2 * p + 2) * tq]], axis=0)
            col0 = (kh * B_REP + 2 * p) * hd
            o_ref[:, col0:col0 + LANES] = pair.T.astype(o_ref.dtype)


def _swa_prompt(qt_all, kv_all, sinks, *, bsz, t_len, tq=256):
    nq = t_len // tq
    hq = B_HEADS * B_HEAD_DIM
    nk = B_KV_HEADS * B_HEAD_DIM
    assert t_len % tq == 0 and t_len >= B_WINDOW + tq and tq % LANES == 0 and B_WINDOW % LANES == 0
    return pl.pallas_call(
        functools.partial(_swa_prompt_body, tq=tq, t_len=t_len),
        grid=(bsz, nq),
        in_specs=[
            pl.BlockSpec(memory_space=pltpu.SMEM),
            pl.BlockSpec((hq, tq), lambda b, i: (0, b * nq + i)),
            pl.BlockSpec((t_len, kv_all.shape[1]), lambda b, i: (b, 0)),
        ],
        out_specs=pl.BlockSpec((tq, hq), lambda b, i: (b * nq + i, 0)),
        out_shape=jax.ShapeDtypeStruct((bsz * t_len, hq), _MXU_DTYPE),
        scratch_shapes=[pltpu.VMEM((t_len, nk), _MXU_DTYPE), pltpu.VMEM((t_len // LANES, nk, LANES), _MXU_DTYPE)],
        compiler_params=_cparams(("arbitrary", "arbitrary"), 40),
    )(sinks, qt_all, kv_all)


def _swa_sample_body(qbd_ref, ckv_ref, nkv_ref, lane_ref, o_ref, *, bt, past, wbuf, tk):
    hd = B_HEAD_DIM
    nk = B_KV_HEADS * hd
    scale2 = hd ** -0.5 * LOG2E
    lanes = qbd_ref.shape[2]
    slope2 = lane_ref[0:1, :]
    sink2 = lane_ref[1:2, :]
    qpos = past + lane_ref[2:3, :].astype(jnp.int32)
    kpos = past - wbuf + lax.broadcasted_iota(jnp.int32, (tk, 1), 0)
    dw = qpos - kpos
    distm = jnp.where((dw >= 0) & (dw < B_WINDOW), dw.astype(_F32), MASKED_DIST)
    lane_kh = _idiv(lax.broadcasted_iota(jnp.int32, (hd, lanes), 1), lanes // B_KV_HEADS)
    pad = jnp.zeros((tk - wbuf - nkv_ref.shape[1], 2 * nk), _F32)
    for bi in range(bt):
        kv = jnp.concatenate([ckv_ref[bi], nkv_ref[bi], pad], axis=0)
        s_t = _dot(kv[:, :nk].astype(_MXU_DTYPE), qbd_ref[bi])
        e, _, l = _softmax_cols(s_t, distm, slope2, scale2, sink2=sink2)
        p_t = (e * (1.0 / l)).astype(_MXU_DTYPE)
        o_full = _dot(kv[:, nk:].T.astype(_MXU_DTYPE), p_t)
        o = jnp.zeros((hd, lanes), _F32)
        for kh in range(B_KV_HEADS):
            o = jnp.where(lane_kh == kh, o_full[kh * hd:(kh + 1) * hd, :], o)
        o_ref[bi] = o


def _swa_sample(qbd, ckv, nkv, lane_consts, *, past, bt=16):
    dbsz, nk, lanes = qbd.shape
    wbuf = ckv.shape[1]
    tk = -(-(wbuf + nkv.shape[1]) // LANES) * LANES
    bt = _pick_tile(dbsz, bt)
    return pl.pallas_call(
        functools.partial(_swa_sample_body, bt=bt, past=past, wbuf=wbuf, tk=tk),
        grid=(dbsz // bt,),
        in_specs=[
            pl.BlockSpec((bt,) + qbd.shape[1:], lambda b: (b, 0, 0)),
            pl.BlockSpec((bt,) + ckv.shape[1:], lambda b: (b, 0, 0)),
            pl.BlockSpec((bt,) + nkv.shape[1:], lambda b: (b, 0, 0)),
            pl.BlockSpec(lane_consts.shape, lambda b: (0, 0)),
        ],
        out_specs=pl.BlockSpec((bt, B_HEAD_DIM, lanes), lambda b: (b, 0, 0)),
        out_shape=jax.ShapeDtypeStruct((dbsz, B_HEAD_DIM, lanes), _F32),
        compiler_params=_cparams(("arbitrary",), 40),
    )(qbd, ckv, nkv, lane_consts)


def _overlap(n_cmp, n_sel, rows, cols):
    start = np.arange(n_cmp) * CMP_STRIDE
    end = start + CMP_LEN - 1
    s0 = np.arange(n_sel) * SEL_BLOCK
    s1 = s0 + SEL_BLOCK - 1
    m = np.zeros((rows, cols), np.float32)
    m[:n_cmp, :n_sel] = (start[:, None] <= s1[None, :]) & (end[:, None] >= s0[None, :])
    return m


def _expand_mat_t(n_blk, t_len, chunk):
    key_blk = np.arange(t_len) // SEL_BLOCK
    m = (key_blk[:, None] == np.arange(n_blk)[None, :]).astype(np.float32)
    return jnp.asarray(m.reshape(t_len // chunk, chunk, n_blk), dtype=_MXU_DTYPE)


def kernel(x_prompt, x_sample, cache_nsa_kv, cache_nsa_win, cache_shared_win, page_table,
           norm_attn, norm_mlp, a_w_in, a_q_gain, a_k_gain, a_cmp_pe, a_cmp_w1, a_cmp_w2, a_w_out,
           kv_norm, kv_w, kv_k_gain, b_w_q, b_q_gain, b_sinks, b_w_out, mlp_w1, mlp_w2):
    bsz, t_len, dm = x_prompt.shape
    dbsz, t_new, _ = x_sample.shape
    n_a = a_w_in.shape[0]
    depth = norm_attn.shape[0]
    n_pool, page_len = cache_nsa_kv.shape[1:3]
    past = page_table.shape[1] * page_len
    d, g, rep = A_HEAD_DIM, A_KV_GROUPS, A_REP
    hq = A_HEADS * d
    hkv = 6 * g * d
    mp = bsz * t_len
    ms = dbsz * t_new
    wdt = _MXU_DTYPE
    slopes_a = jnp.asarray(_alibi_slopes(A_HEADS))

    hs2 = [x_prompt.reshape(mp, dm), x_sample.reshape(ms, dm)]
    both = lambda fn: [fn(hh) for hh in hs2]
    rows_p, rows_s, win_p, win_s = [], [], [], []
    kv_sh = None
    for layer in range(depth):
        if layer < n_a:
            a = layer
            w_in = a_w_in[a]
            wq = w_in[:, :hq].astype(wdt)
            qt_p, qt_s = both(lambda hh: _rms_mm(hh, norm_attn[layer], wq, epi="headnorm", transpose_out=True,
                                                 gain=jnp.tile(a_q_gain[a], A_HEADS), flag=jnp.ones((hq,), _F32),
                                                 out_dtype=wdt))
            one, zero = jnp.ones((g * d,), _F32), jnp.zeros((g * d,), _F32)
            kgain = a_k_gain[a]
            kv_gain = jnp.concatenate([one, one, jnp.tile(kgain[1], g), one, jnp.tile(kgain[2], g), one])
            kv_flag = jnp.concatenate([zero, zero, one, zero, one, zero])
            wkv = w_in[:, hq:hq + hkv].astype(wdt)
            kv_proj = functools.partial(_rms_mm, gamma=norm_attn[layer], w=wkv, epi="headnorm", gain=kv_gain, flag=kv_flag,
                                        split=(4 * g * d, 2 * g * d))
            rw_p, wn_p, rw_il = kv_proj(hs2[0], interleave=4 * g)
            rw_s, wn_s = kv_proj(hs2[1])
            n_gate = 3 * A_HEADS
            w_gate = jnp.pad(w_in[:, hq + hkv:], ((0, 0), (0, LANES - n_gate))).astype(wdt)
            gates_p, gates_s = both(lambda hh: _rms_mm(hh, norm_attn[layer], w_gate, epi="sigmoid")[:, :n_gate])

            w1cat = (a_cmp_w1[a].reshape(2, CMP_RATIO, CMP_STRIDE, d, d).transpose(0, 2, 3, 1, 4)
                     .reshape(2, CMP_STRIDE * d, CMP_RATIO * d).astype(wdt))
            pe2 = jnp.pad(a_cmp_pe[a].reshape(2, CMP_RATIO, CMP_STRIDE * d),
                          ((0, 0), (0, SUBLANES - CMP_RATIO), (0, 0))).astype(wdt)
            w2 = a_cmp_w2[a].astype(wdt)
            kgain0 = kgain[0].reshape(1, d)

            kc, vct = _cmp_prompt(rw_p, w1cat, pe2, w2, kgain0, bsz=bsz, t_len=t_len)
            n_chunk = t_len // CMP_STRIDE
            n_sel_p = -(-t_len // SEL_BLOCK)
            n_blk_p = -(-n_sel_p // SUBLANES) * SUBLANES
            ovt_p = jnp.asarray(_overlap(n_chunk - CMP_RATIO + 1, n_sel_p, n_chunk, n_blk_p).T)
            chunk = 512
            et = _expand_mat_t(n_blk_p, t_len, chunk)
            gates_t = (gates_p.reshape(bsz, t_len, 3, g, rep).transpose(0, 3, 2, 4, 1)
                       .reshape(bsz, g, 3 * rep, t_len))
            o_p = _nsa_prompt(qt_p, rw_p, wn_p, gates_t, kc, vct, ovt_p, et, slopes_a,
                              bsz=bsz, t_len=t_len, chunk=chunk)

            q_s = (qt_s.astype(_F32).reshape(g, rep, d, dbsz, t_new).transpose(3, 0, 4, 1, 2)
                   .reshape(dbsz, g, t_new * rep, d))
            gt_s = (gates_s.reshape(dbsz, t_new, 3, g, rep).transpose(0, 3, 1, 4, 2)
                    .reshape(dbsz, g, t_new * rep, 3))
            kv_s = jnp.concatenate([rw_s, wn_s], axis=1).reshape(dbsz, t_new, 6, g, d)
            new_rows = lambda slot: jnp.pad(kv_s[:, :, slot].transpose(0, 2, 1, 3),
                                            ((0, 0), (0, 0), (0, SUBLANES - t_new), (0, 0)))
            cache_rows = cache_nsa_kv[a].reshape(n_pool, page_len * 4 * g, d)
            cwin = cache_nsa_win[a]
            wbuf = cwin.shape[1]
            n_chunk_s = past // CMP_STRIDE
            n_sel_s = -(-(past + t_new) // SEL_BLOCK)
            n_blk_l = -(-n_sel_s // LANES) * LANES
            ov_s = jnp.asarray(_overlap(n_chunk_s - CMP_RATIO + 1, n_sel_s, n_chunk_s, n_blk_l))
            o_s = _nsa_sample(page_table, slopes_a, cache_rows, q_s, gt_s, new_rows(2), new_rows(3), new_rows(4),
                              new_rows(5), cwin.reshape(dbsz, wbuf * 2 * g, d), w1cat, pe2, w2, kgain0, ov_s,
                              past=past, t_new=t_new, page_len=page_len)
            o_s = o_s.reshape(dbsz, g, t_new, rep, d).transpose(0, 2, 1, 3, 4).reshape(ms, hq).astype(wdt)
            w_out = a_w_out[a]

            rows_p.append(rw_il.reshape(bsz, t_len, 4, g, d))
            rows_s.append(rw_s.reshape(dbsz, t_new, 4, g, d))
            win_p.append(wn_p.reshape(bsz, t_len, 2, g, d)[:, -min(A_WINDOW, t_len):])
            win_s.append(jnp.concatenate([cwin, wn_s.reshape(dbsz, t_new, 2, g, d)], axis=1)[:, -wbuf:])
        else:
            bl = layer - n_a
            hb = B_HEADS * B_HEAD_DIM
            nk = B_KV_HEADS * B_HEAD_DIM
            wq = b_w_q[bl].astype(wdt)
            qt_p, qt_s = both(lambda hh: _rms_mm(hh, norm_attn[layer], wq, epi="headnorm", hd=B_HEAD_DIM,
                                                 transpose_out=True, gain=jnp.tile(b_q_gain[bl], B_HEADS),
                                                 flag=jnp.ones((hb,), _F32), out_dtype=wdt))
            o_p = _swa_prompt(qt_p, kv_sh[0], b_sinks[bl], bsz=bsz, t_len=t_len)

            q5 = (qt_s.astype(_F32).reshape(B_KV_HEADS, B_REP, B_HEAD_DIM, dbsz, t_new)
                  .transpose(3, 0, 2, 1, 4).reshape(dbsz, B_KV_HEADS, B_HEAD_DIM, B_REP * t_new))
            qbd = (q5[:, :, :, None, :] * jnp.eye(B_KV_HEADS, dtype=_F32)[None, :, None, :, None]
                   ).reshape(dbsz, nk, B_HEADS * t_new).astype(wdt)
            slopes_b = jnp.asarray(_alibi_slopes(B_HEADS))
            lane_consts = jnp.stack([jnp.repeat(slopes_b * LOG2E, t_new), jnp.repeat(b_sinks[bl] * LOG2E, t_new),
                                     jnp.tile(jnp.arange(t_new, dtype=_F32), B_HEADS)]
                                    + [jnp.zeros((B_HEADS * t_new,), _F32)] * (SUBLANES - 3))
            nkv = jnp.pad(kv_sh[1].reshape(dbsz, t_new, 2 * nk), ((0, 0), (0, SUBLANES - t_new), (0, 0)))
            ckv = cache_shared_win.reshape(dbsz, cache_shared_win.shape[1], 2 * nk)
            o_s = _swa_sample(qbd, ckv, nkv, lane_consts, past=past)
            o_s = (o_s.reshape(dbsz, B_HEAD_DIM, B_HEADS, t_new).transpose(0, 3, 2, 1).reshape(ms, hb).astype(wdt))
            w_out = b_w_out[bl]

        w_o = w_out.astype(wdt)
        h_s, w_1, w_2 = _mlp_cast(_mm_res(o_s, w_o, hs2[1]), norm_mlp[layer], mlp_w1[layer], mlp_w2[layer])
        hs2 = [_mlp(_mm_res(o_p, w_o, hs2[0]), norm_mlp[layer], w_1, w_2), h_s]
        if layer == n_a - 1:
            nk = B_KV_HEADS * B_HEAD_DIM
            wkv_sh = kv_w.astype(wdt)
            kv_sh = both(lambda hh: _rms_mm(
                hh, kv_norm, wkv_sh, epi="headnorm", hd=B_HEAD_DIM,
                gain=jnp.concatenate([jnp.tile(kv_k_gain, B_KV_HEADS), jnp.ones((nk,), _F32)]),
                flag=jnp.concatenate([jnp.ones((nk,), _F32), jnp.zeros((nk,), _F32)])))

    wb = cache_shared_win.shape[1]
    kv_p = kv_sh[0].reshape(bsz, t_len, 2, B_KV_HEADS, B_HEAD_DIM)
    kv_s_new = kv_sh[1].reshape(dbsz, t_new, 2, B_KV_HEADS, B_HEAD_DIM)
    return (hs2[0].reshape(bsz, t_len, dm), hs2[1].reshape(dbsz, t_new, dm),
            jnp.stack(rows_p), jnp.stack(rows_s), jnp.stack(win_p), jnp.stack(win_s),
            kv_p[:, -min(B_WINDOW, t_len):], jnp.concatenate([cache_shared_win, kv_s_new], axis=1)[:, -wb:])
```
